```python
import jax, jax.numpy as jnp
from jax import lax
import numpy as np

D_MODEL = 1024
BATCH = 8
SEQ = 2048
DEPTH = 2

N_EVEN = (DEPTH + 1) // 2
N_ODD = DEPTH // 2
EPS = 1e-6

CONV_CH = 512
CONV_WIDTH = 31
HEAD_DIM = 64
HEADS_PER_GROUP = 8
DILATED_PAIRS = ((128, 1), (512, 4), (2048, 16))
N_GROUPS = len(DILATED_PAIRS)
ATTN_HEADS = N_GROUPS * HEADS_PER_GROUP
ATTN_WIDTH = ATTN_HEADS * HEAD_DIM
ATTN_OUT = HEADS_PER_GROUP * HEAD_DIM
ROPE_THETA = 10000.0
EVEN_IN = 2 * CONV_CH + 3 * ATTN_WIDTH
EVEN_OUT = CONV_CH + ATTN_OUT
SCONV_CH = 512
SCONV_WIDTH = 3
SG_GROUPS = 4
SG_HEAD = 128
SG_CH = SG_GROUPS * SG_HEAD
CHUNK = 128
ODD_IN = 3 * SCONV_CH + 2 * SG_CH
ODD_OUT = SCONV_CH + SG_CH
D_FF = 4 * D_MODEL

kernel_name = "hybrid_conformer_dilated_shortconv_gmlp_trunk"


def rms_norm(x, g):
    xf = x.astype(jnp.float32)
    y = xf * lax.rsqrt(jnp.mean(xf * xf, axis=-1, keepdims=True) + EPS)
    return (y * g.astype(jnp.float32)).astype(x.dtype)


def layer_norm(x, g, b):
    xf = x.astype(jnp.float32)
    mu = jnp.mean(xf, axis=-1, keepdims=True)
    xc = xf - mu
    y = xc * lax.rsqrt(jnp.mean(xc * xc, axis=-1, keepdims=True) + EPS)
    return (y * g.astype(jnp.float32) + b.astype(jnp.float32)).astype(x.dtype)


def rope_tables(seq):
    half = HEAD_DIM // 2
    inv = ROPE_THETA ** (-jnp.arange(half, dtype=jnp.float32) / half)
    ang = jnp.arange(seq, dtype=jnp.float32)[:, None] * inv[None, :]
    return jnp.cos(ang), jnp.sin(ang)


def apply_rope(x, cos, sin):
    x1, x2 = jnp.split(x, 2, axis=-1)
    c = cos[None, :, None, :]
    s = sin[None, :, None, :]
    return jnp.concatenate([x1 * c - x2 * s, x2 * c + x1 * s], axis=-1).astype(x.dtype)


def causal_depthwise_conv(x, kern):
    w = kern.shape[0]
    return lax.conv_general_dilated(
        x, kern[:, None, :].astype(x.dtype), window_strides=(1,),
        padding=[(w - 1, 0)], dimension_numbers=('NWC', 'WIO', 'NWC'),
        feature_group_count=x.shape[-1])


def banded_causal_attention(q, k, v, band):
    b, r, l, h, dh = q.shape
    nb = -(-l // band)
    lp = nb * band
    pad = ((0, 0), (0, 0), (0, lp - l), (0, 0), (0, 0))
    qb = jnp.pad(q, pad).reshape(b, r, nb, band, h, dh)
    kb = jnp.pad(k, pad).reshape(b, r, nb, band, h, dh)
    vb = jnp.pad(v, pad).reshape(b, r, nb, band, h, dh)
    zk = jnp.zeros_like(kb[:, :, :1])
    k2 = jnp.concatenate([jnp.concatenate([zk, kb[:, :, :-1]], axis=2), kb], axis=3)
    v2 = jnp.concatenate([jnp.concatenate([zk, vb[:, :, :-1]], axis=2), vb], axis=3)
    s = jnp.einsum('brnqhd,brnkhd->brnhqk', qb, k2,
                   preferred_element_type=jnp.float32) * (HEAD_DIM ** -0.5)
    qi = jnp.arange(band)[:, None] + band
    ki = jnp.arange(2 * band)[None, :]
    dist = qi - ki
    local = (dist >= 0) & (dist <= band)
    kvalid = (jnp.arange(nb)[:, None] * band - band + ki) >= 0
    mask = local[None, :, :] & kvalid[:, None, :]
    s = jnp.where(mask[None, None, :, None], s, -jnp.inf)
    lse = jax.nn.logsumexp(s, axis=-1, keepdims=True)
    p = jnp.exp(s - lse)
    o = jnp.einsum('brnhqk,brnkhd->brnqhd', p, v2.astype(jnp.float32))
    o = o.reshape(b, r, lp, h, dh)[:, :, :l]
    lse = lse[..., 0].transpose(0, 1, 2, 4, 3).reshape(b, r, lp, h)[:, :, :l]
    return o, lse


def dilated_group_attention(q, k, v, window, dilation):
    b, s, h, dh = q.shape
    l = s // dilation

    def to_res(t):
        return t.reshape(b, l, dilation, h, dh).transpose(0, 2, 1, 3, 4)

    o, lse = banded_causal_attention(to_res(q), to_res(k), to_res(v), window // dilation)
    o = o.transpose(0, 2, 1, 3, 4).reshape(b, s, h, dh)
    lse = lse.transpose(0, 2, 1, 3).reshape(b, s, h)
    return o, lse


def even_mixer(h, w_in, conv_k, conv_b, ln_g, ln_b, w_out, cos, sin):
    b, s, _ = h.shape
    z = h @ w_in
    a_lin, a_gate, qkv = jnp.split(z, [CONV_CH, 2 * CONV_CH], axis=-1)
    a = a_lin * jax.nn.sigmoid(a_gate)
    a = causal_depthwise_conv(a, conv_k) + conv_b.astype(a.dtype)
    a = jax.nn.silu(layer_norm(a, ln_g, ln_b))
    q, k, v = jnp.split(qkv, 3, axis=-1)
    q = apply_rope(q.reshape(b, s, ATTN_HEADS, HEAD_DIM), cos, sin)
    k = apply_rope(k.reshape(b, s, ATTN_HEADS, HEAD_DIM), cos, sin)
    v = v.reshape(b, s, ATTN_HEADS, HEAD_DIM)
    outs, lses = [], []
    for g, (window, dilation) in enumerate(DILATED_PAIRS):
        sl = slice(g * HEADS_PER_GROUP, (g + 1) * HEADS_PER_GROUP)
        o, lse = dilated_group_attention(q[:, :, sl], k[:, :, sl], v[:, :, sl], window, dilation)
        outs.append(o)
        lses.append(lse)
    wts = jax.nn.softmax(jnp.stack(lses, axis=0), axis=0)
    att = jnp.sum(wts[..., None] * jnp.stack(outs, axis=0), axis=0)
    att = att.reshape(b, s, ATTN_OUT).astype(a.dtype)
    return jnp.concatenate([a, att], axis=-1) @ w_out


def odd_mixer(h, w_in, sconv_k, sg_ln_g, sg_ln_b, sg_w, sg_b, w_out):
    b, s, _ = h.shape
    z = h @ w_in
    gb, gc, xs, uv = jnp.split(z, [SCONV_CH, 2 * SCONV_CH, 3 * SCONV_CH], axis=-1)
    c_out = gb * causal_depthwise_conv(gc * xs, sconv_k)
    u, v = jnp.split(jax.nn.gelu(uv), 2, axis=-1)
    v = layer_norm(v, sg_ln_g, sg_ln_b)
    v = v.reshape(b, s // CHUNK, CHUNK, SG_GROUPS, SG_HEAD)
    ws = sg_w * jnp.tril(jnp.ones((CHUNK, CHUNK), dtype=sg_w.dtype))[None]
    v = jnp.einsum('gts,bnsgc->bntgc', ws.astype(v.dtype), v) + sg_b.T.astype(v.dtype)[None, None, :, :, None]
    d_out = u * v.reshape(b, s, SG_CH)
    return jnp.concatenate([c_out, d_out], axis=-1) @ w_out


def channel_mixer(h, w1, w2):
    return jnp.square(jax.nn.relu(h @ w1)) @ w2


def setup_inputs(seed: int = 0) -> dict:
    key = jax.random.key(seed)
    ks = jax.random.split(key, 20)
    f32 = jnp.float32

    def nrm(k, shape, scale):
        return jax.random.normal(k, shape, f32) * scale

    return {
        "x": nrm(ks[0], (BATCH, SEQ, D_MODEL), 1.0),
        "norm_mix_g": 1.0 + nrm(ks[1], (DEPTH, D_MODEL), 0.02),
        "norm_ffn_g": 1.0 + nrm(ks[2], (DEPTH, D_MODEL), 0.02),
        "even_w_in": nrm(ks[3], (N_EVEN, D_MODEL, EVEN_IN), D_MODEL ** -0.5),
        "even_conv_k": nrm(ks[4], (N_EVEN, CONV_WIDTH, CONV_CH), CONV_WIDTH ** -0.5),
        "even_conv_b": nrm(ks[5], (N_EVEN, CONV_CH), 0.02),
        "even_ln_g": 1.0 + nrm(ks[6], (N_EVEN, CONV_CH), 0.02),
        "even_ln_b": nrm(ks[7], (N_EVEN, CONV_CH), 0.02),
        "even_w_out": nrm(ks[8], (N_EVEN, EVEN_OUT, D_MODEL), EVEN_OUT ** -0.5),
        "odd_w_in": nrm(ks[9], (N_ODD, D_MODEL, ODD_IN), D_MODEL ** -0.5),
        "odd_conv_k": nrm(ks[10], (N_ODD, SCONV_WIDTH, SCONV_CH), SCONV_WIDTH ** -0.5),
        "odd_ln_g": 1.0 + nrm(ks[11], (N_ODD, SG_CH), 0.02),
        "odd_ln_b": nrm(ks[12], (N_ODD, SG_CH), 0.02),
        "odd_sg_w": nrm(ks[13], (N_ODD, SG_GROUPS, CHUNK, CHUNK), CHUNK ** -0.5),
        "odd_sg_b": 1.0 + nrm(ks[14], (N_ODD, SG_GROUPS, CHUNK), 0.02),
        "odd_w_out": nrm(ks[15], (N_ODD, ODD_OUT, D_MODEL), ODD_OUT ** -0.5),
        "ffn_w1": nrm(ks[16], (DEPTH, D_MODEL, D_FF), D_MODEL ** -0.5),
        "ffn_w2": nrm(ks[17], (DEPTH, D_FF, D_MODEL), D_FF ** -0.5),
        "final_g": 1.0 + nrm(ks[18], (D_MODEL,), 0.02),
    }


def reference(x, norm_mix_g, norm_ffn_g, even_w_in, even_conv_k, even_conv_b, even_ln_g,
              even_ln_b, even_w_out, odd_w_in, odd_conv_k, odd_ln_g, odd_ln_b, odd_sg_w,
              odd_sg_b, odd_w_out, ffn_w1, ffn_w2, final_g):
    cos, sin = rope_tables(x.shape[1])
    h = x
    for i in range(DEPTH):
        hn = rms_norm(h, norm_mix_g[i])
        if i % 2 == 0:
            j = i // 2
            mix = even_mixer(hn, even_w_in[j], even_conv_k[j], even_conv_b[j], even_ln_g[j],
                             even_ln_b[j], even_w_out[j], cos, sin)
        else:
            j = i // 2
            mix = odd_mixer(hn, odd_w_in[j], odd_conv_k[j], odd_ln_g[j], odd_ln_b[j],
                            odd_sg_w[j], odd_sg_b[j], odd_w_out[j])
        h = h + mix.astype(h.dtype)
        h = h + channel_mixer(rms_norm(h, norm_ffn_g[i]), ffn_w1[i], ffn_w2[i]).astype(h.dtype)
    return rms_norm(h, final_g)
```

```python
import functools
import math

import jax
import jax.numpy as jnp
from jax import lax
from jax.experimental import pallas as pl
from jax.experimental.pallas import tpu as pltpu

F32 = jnp.float32
BF16 = jnp.bfloat16

EPS = 1e-6
LANES = 128
CONV_CH = 512
CONV_WIDTH = 31
HEAD_DIM = 64
HEADS_PER_GROUP = 8
DILATED_PAIRS = ((128, 1), (512, 4), (2048, 16))
N_GROUPS = len(DILATED_PAIRS)
GROUP_WIDTH = HEADS_PER_GROUP * HEAD_DIM
ATTN_WIDTH = N_GROUPS * GROUP_WIDTH
ROPE_THETA = 10000.0
SCONV_CH = 512
SCONV_WIDTH = 3
SG_GROUPS = 4
SG_HEAD = 128
SG_CH = SG_GROUPS * SG_HEAD
CHUNK = 128
BAND = 128
NEG_BIG = -1e30

VMEM_LIMIT = 56 * 1024 * 1024


def _rms(x, g):
    return x * lax.rsqrt(jnp.mean(x * x, axis=-1, keepdims=True) + EPS) * g


def _layer_norm(x, g, b):
    mu = jnp.mean(x, axis=-1, keepdims=True)
    xc = x - mu
    return xc * lax.rsqrt(jnp.mean(xc * xc, axis=-1, keepdims=True) + EPS) * g + b


def _const_spec(shape):
    return pl.BlockSpec(shape, lambda *_: (0,) * len(shape))


def _params(sem):
    return pltpu.CompilerParams(dimension_semantics=sem, vmem_limit_bytes=VMEM_LIMIT)


def _in0_kernel(x_ref, g_ref, w_ref, cos_ref, sin_ref, glu_ref, q_ref, k_ref, v_ref):
    tm = x_ref.shape[0]
    hn = _rms(x_ref[...], g_ref[...]).astype(BF16)

    def proj(c0, n):
        return jnp.dot(hn, w_ref[:, c0:c0 + n], preferred_element_type=F32)

    lin = proj(0, CONV_CH)
    gate = proj(CONV_CH, CONV_CH)
    glu_ref[...] = lin * jax.nn.sigmoid(gate)

    cos = cos_ref[...]
    sin = sin_ref[...]
    lane = lax.broadcasted_iota(jnp.int32, (tm, LANES), 1)
    first_half = (lane % HEAD_DIM) < (HEAD_DIM // 2)

    def rope(z):
        rot = jnp.where(first_half,
                        pltpu.roll(z, LANES - HEAD_DIM // 2, 1),
                        pltpu.roll(z, HEAD_DIM // 2, 1))
        return z * cos + rot * sin

    base = 2 * CONV_CH
    for c in range(N_GROUPS):
        zq = proj(base + c * GROUP_WIDTH, GROUP_WIDTH)
        zk = proj(base + ATTN_WIDTH + c * GROUP_WIDTH, GROUP_WIDTH)
        for s in range(GROUP_WIDTH // LANES):
            lo = c * GROUP_WIDTH + s * LANES
            q_ref[:, lo:lo + LANES] = rope(zq[:, s * LANES:(s + 1) * LANES]) * (HEAD_DIM ** -0.5)
            k_ref[:, lo:lo + LANES] = rope(zk[:, s * LANES:(s + 1) * LANES])
        v_ref[:, c * GROUP_WIDTH:(c + 1) * GROUP_WIDTH] = proj(
            base + 2 * ATTN_WIDTH + c * GROUP_WIDTH, GROUP_WIDTH)


def _in0_call(x2, g, w_bf, cos_t, sin_t, seq, tm):
    t, d = x2.shape
    n_in = w_bf.shape[1]
    blocks_per_seq = seq // tm
    row = lambda i: (i, 0)
    return pl.pallas_call(
        _in0_kernel,
        grid=(t // tm,),
        in_specs=[
            pl.BlockSpec((tm, d), row),
            _const_spec((1, d)),
            _const_spec((d, n_in)),
            pl.BlockSpec((tm, LANES), lambda i: (i % blocks_per_seq, 0)),
            pl.BlockSpec((tm, LANES), lambda i: (i % blocks_per_seq, 0)),
        ],
        out_specs=[
            pl.BlockSpec((tm, CONV_CH), row),
            pl.BlockSpec((tm, ATTN_WIDTH), row),
            pl.BlockSpec((tm, ATTN_WIDTH), row),
            pl.BlockSpec((tm, ATTN_WIDTH), row),
        ],
        out_shape=[
            jax.ShapeDtypeStruct((t, CONV_CH), F32),
            jax.ShapeDtypeStruct((t, ATTN_WIDTH), F32),
            jax.ShapeDtypeStruct((t, ATTN_WIDTH), F32),
            jax.ShapeDtypeStruct((t, ATTN_WIDTH), F32),
        ],
        compiler_params=_params(("parallel",)),
        name="l0_in_proj",
    )(x2, g, w_bf, cos_t, sin_t)


CONV_HALO = 32
CONV_ROWS = 64


def _conv_kernel(prev_ref, cur_ref, k_ref, b_ref, lg_ref, lb_ref, out_ref, buf):
    tr = cur_ref.shape[0]
    i = pl.program_id(1)
    buf[0:CONV_HALO, :] = jnp.where(i > 0, prev_ref[...], 0.0)
    buf[CONV_HALO:CONV_HALO + tr, :] = cur_ref[...]
    bias = b_ref[...]
    lg = lg_ref[...]
    lb = lb_ref[...]
    off = CONV_HALO - (CONV_WIDTH - 1)
    for r0 in range(0, tr, CONV_ROWS):
        acc = jnp.zeros((CONV_ROWS, CONV_CH), F32)
        for j in range(CONV_WIDTH):
            acc = acc + buf[r0 + off + j:r0 + off + j + CONV_ROWS, :] * k_ref[j:j + 1, :]
        y = _layer_norm(acc + bias, lg, lb)
        out_ref[r0:r0 + CONV_ROWS, :] = y * jax.nn.sigmoid(y)


def _conv_call(glu3, conv_k, conv_b, ln_g, ln_b, tr):
    b, s, c = glu3.shape
    halo_blocks = tr // CONV_HALO
    return pl.pallas_call(
        _conv_kernel,
        grid=(b, s // tr),
        in_specs=[
            pl.BlockSpec((None, CONV_HALO, c),
                         lambda bi, i: (bi, jnp.maximum(i * halo_blocks - 1, 0), 0)),
            pl.BlockSpec((None, tr, c), lambda bi, i: (bi, i, 0)),
            _const_spec((CONV_WIDTH, c)),
            _const_spec((1, c)),
            _const_spec((1, c)),
            _const_spec((1, c)),
        ],
        out_specs=pl.BlockSpec((None, tr, c), lambda bi, i: (bi, i, 0)),
        out_shape=jax.ShapeDtypeStruct((b, s, c), F32),
        scratch_shapes=[pltpu.VMEM((CONV_HALO + tr, c), F32)],
        compiler_params=_params(("parallel", "parallel")),
        name="l0_conv_ln_swish",
    )(glu3, glu3, conv_k, conv_b, ln_g, ln_b)


def _attn_kernel(q0, k0, v0, q1, k1, v1, q2, k2, v2, out_ref, acc_o, acc_m, acc_l):
    seq = out_ref.shape[0]
    lane = lax.broadcasted_iota(jnp.int32, (BAND, LANES), 1)
    head0 = lane < HEAD_DIM
    row = lax.broadcasted_iota(jnp.int32, (BAND, 2 * BAND), 0)
    col = lax.broadcasted_iota(jnp.int32, (BAND, 2 * BAND), 1)
    allowed2 = jnp.logical_or(jnp.logical_and(col < BAND, col >= row),
                              jnp.logical_and(col >= BAND, (col - BAND) <= row))
    bias2 = jnp.where(allowed2, 0.0, NEG_BIG).astype(F32)
    bias_noprev = jnp.where(col < BAND, NEG_BIG, 0.0).astype(F32)
    row1 = lax.broadcasted_iota(jnp.int32, (BAND, BAND), 0)
    col1 = lax.broadcasted_iota(jnp.int32, (BAND, BAND), 1)
    bias1 = jnp.where(col1 <= row1, 0.0, NEG_BIG).astype(F32)

    def block(q, kk, vv, bias):
        qb = q.astype(BF16)
        kb = kk.astype(BF16)
        vb = vv.astype(BF16)
        outs = []
        for h in range(2):
            sel = head0 if h == 0 else jnp.logical_not(head0)
            qh = jnp.where(sel, qb, jnp.zeros_like(qb))
            s = lax.dot_general(qh, kb, (((1,), (1,)), ((), ())),
                                preferred_element_type=F32) + bias
            m = jnp.max(s, axis=-1, keepdims=True)
            p = jnp.exp(s - m)
            l = jnp.sum(p, axis=-1, keepdims=True)
            o = jnp.dot(p.astype(BF16), vb, preferred_element_type=F32)
            outs.append((o, m, l))
        (o_a, m_a, l_a), (o_b, m_b, l_b) = outs
        o = jnp.where(head0, o_a, o_b)
        m = jnp.where(head0, m_a, m_b)
        l = jnp.where(head0, l_a, l_b)
        return o, m, l

    groups = ((q0, k0, v0), (q1, k1, v1), (q2, k2, v2))
    for g, (window, dil) in enumerate(DILATED_PAIRS):
        q_ref, k_ref, v_ref = groups[g]
        nb = (seq // dil) // BAND
        n_iter = dil * nb
        span = dil * BAND

        def rows(ref, start, dil=dil):
            if dil == 1:
                return ref[pl.ds(start, BAND), :]
            return ref[pl.ds(start, BAND, stride=dil), :]

        def body(it, carry, g=g, dil=dil, nb=nb, span=span,
                 q_ref=q_ref, k_ref=k_ref, v_ref=v_ref, rows=rows):
            r = it // nb
            n = it % nb
            start = r + n * span
            q = rows(q_ref, start)
            kc = rows(k_ref, start)
            vc = rows(v_ref, start)
            if nb == 1:
                o, m, l = block(q, kc, vc, bias1)
            else:
                pstart = jnp.maximum(start - span, 0)
                kp = rows(k_ref, pstart)
                vp = rows(v_ref, pstart)
                bias = bias2 + jnp.where(n > 0, 0.0, 1.0) * bias_noprev
                o, m, l = block(q, jnp.concatenate([kp, kc], axis=0),
                                jnp.concatenate([vp, vc], axis=0), bias)
            if g == 0:
                acc_o[pl.ds(start, BAND), :] = o
                acc_m[pl.ds(start, BAND), :] = m
                acc_l[pl.ds(start, BAND), :] = l
            else:
                idx = pl.ds(start, BAND, stride=dil)
                m_old = acc_m[idx, :]
                m_new = jnp.maximum(m_old, m)
                a_old = jnp.exp(m_old - m_new)
                a_new = jnp.exp(m - m_new)
                acc_o[idx, :] = acc_o[idx, :] * a_old + o * a_new
                acc_l[idx, :] = acc_l[idx, :] * a_old + l * a_new
                acc_m[idx, :] = m_new
            return carry

        lax.fori_loop(0, n_iter, body, 0)

    out_ref[...] = acc_o[...] / acc_l[...]


def _attn_call(q3, k3, v3):
    b, s, _ = q3.shape
    pairs = GROUP_WIDTH // LANES
    in_specs = []
    args = []
    for g in range(N_GROUPS):
        for arr in (q3, k3, v3):
            in_specs.append(pl.BlockSpec((None, s, LANES),
                                         lambda bi, p, g=g: (bi, 0, g * pairs + p)))
            args.append(arr)
    return pl.pallas_call(
        _attn_kernel,
        grid=(b, pairs),
        in_specs=in_specs,
        out_specs=pl.BlockSpec((None, s, LANES), lambda bi, p: (bi, 0, p)),
        out_shape=jax.ShapeDtypeStruct((b, s, GROUP_WIDTH), F32),
        scratch_shapes=[pltpu.VMEM((s, LANES), F32)] * 3,
        compiler_params=_params(("parallel", "parallel")),
        name="l0_dilated_attention",
    )(*args)


def _out0_kernel(x_ref, a_ref, att_ref, w_ref, o_ref):
    c = a_ref.shape[1]
    mix = jnp.dot(a_ref[...].astype(BF16), w_ref[0:c, :], preferred_element_type=F32)
    mix = mix + jnp.dot(att_ref[...].astype(BF16), w_ref[c:, :], preferred_element_type=F32)
    o_ref[...] = x_ref[...] + mix


def _out0_call(x2, a2, att2, w_bf, tm):
    t, d = x2.shape
    row = lambda i: (i, 0)
    return pl.pallas_call(
        _out0_kernel,
        grid=(t // tm,),
        in_specs=[
            pl.BlockSpec((tm, d), row),
            pl.BlockSpec((tm, a2.shape[1]), row),
            pl.BlockSpec((tm, att2.shape[1]), row),
            _const_spec(w_bf.shape),
        ],
        out_specs=pl.BlockSpec((tm, d), row),
        out_shape=jax.ShapeDtypeStruct((t, d), F32),
        compiler_params=_params(("parallel",)),
        name="l0_out_proj",
    )(x2, a2, att2, w_bf)


FFN_CHUNK = 1024


def _ffn_kernel(h_ref, g_ref, w1_ref, w2_ref, fg_ref, o_ref, *, final_norm):
    h = h_ref[...]
    hn = _rms(h, g_ref[...]).astype(BF16)
    d_ff = w1_ref.shape[1]
    acc = h
    for c0 in range(0, d_ff, FFN_CHUNK):
        a = jnp.dot(hn, w1_ref[:, c0:c0 + FFN_CHUNK], preferred_element_type=F32)
        a = jnp.square(jnp.maximum(a, 0.0)).astype(BF16)
        acc = acc + jnp.dot(a, w2_ref[c0:c0 + FFN_CHUNK, :], preferred_element_type=F32)
    if final_norm:
        acc = _rms(acc, fg_ref[...])
    o_ref[...] = acc


def _ffn_call(h2, g, w1_bf, w2_bf, final_g, final_norm, tm):
    t, d = h2.shape
    row = lambda i: (i, 0)
    return pl.pallas_call(
        functools.partial(_ffn_kernel, final_norm=final_norm),
        grid=(t // tm,),
        in_specs=[
            pl.BlockSpec((tm, d), row),
            _const_spec((1, d)),
            _const_spec(w1_bf.shape),
            _const_spec(w2_bf.shape),
            _const_spec((1, d)),
        ],
        out_specs=pl.BlockSpec((tm, d), row),
        out_shape=jax.ShapeDtypeStruct((t, d), F32),
        compiler_params=_params(("parallel",)),
        name="ffn_final" if final_norm else "ffn",
    )(h2, g, w1_bf, w2_bf, final_g)


SCONV_HALO = 8
GELU_C = math.sqrt(2.0 / math.pi)


def _gelu_tanh(x):
    return 0.5 * x * (1.0 + jnp.tanh(GELU_C * (x + 0.044715 * (x * x * x))))


def _l1_kernel(h_ref, g_ref, win_ref, ck_ref, lg_ref, lb_ref, sgw_ref, sgbt_ref, wout_ref,
               o_ref, ybuf, *, blocks_per_seq):
    tm = h_ref.shape[0]
    i = pl.program_id(0)
    h = h_ref[...]
    hn = _rms(h, g_ref[...]).astype(BF16)

    def proj(c0, n):
        return jnp.dot(hn, win_ref[:, c0:c0 + n], preferred_element_type=F32)

    gb = proj(0, SCONV_CH)
    y = proj(SCONV_CH, SCONV_CH) * proj(2 * SCONV_CH, SCONV_CH)

    @pl.when(i % blocks_per_seq == 0)
    def _():
        ybuf[0:SCONV_HALO, :] = jnp.zeros((SCONV_HALO, SCONV_CH), F32)

    ybuf[SCONV_HALO:SCONV_HALO + tm, :] = y
    conv = y * ck_ref[SCONV_WIDTH - 1:SCONV_WIDTH, :]
    for j in range(SCONV_WIDTH - 1):
        sh = SCONV_WIDTH - 1 - j
        conv = conv + ybuf[SCONV_HALO - sh:SCONV_HALO - sh + tm, :] * ck_ref[j:j + 1, :]
    ybuf[0:SCONV_HALO, :] = y[tm - SCONV_HALO:tm, :]
    c_out = (gb * conv).astype(BF16)

    u = _gelu_tanh(proj(3 * SCONV_CH, SG_CH))
    v = _gelu_tanh(proj(3 * SCONV_CH + SG_CH, SG_CH))
    v = _layer_norm(v, lg_ref[...], lb_ref[...]).astype(BF16)
    rr = lax.broadcasted_iota(jnp.int32, (CHUNK, CHUNK), 0)
    cc = lax.broadcasted_iota(jnp.int32, (CHUNK, CHUNK), 1)
    causal = rr >= cc
    d_cols = []
    for gi in range(SG_GROUPS):
        ws = jnp.where(causal, sgw_ref[gi], 0.0).astype(BF16)
        bcol = sgbt_ref[:, gi:gi + 1]
        chunks = []
        for n in range(tm // CHUNK):
            vv = v[n * CHUNK:(n + 1) * CHUNK, gi * SG_HEAD:(gi + 1) * SG_HEAD]
            chunks.append(jnp.dot(ws, vv, preferred_element_type=F32) + bcol)
        d_cols.append(jnp.concatenate(chunks, axis=0))
    d_out = (u * jnp.concatenate(d_cols, axis=1)).astype(BF16)

    mix = jnp.dot(c_out, wout_ref[0:SCONV_CH, :], preferred_element_type=F32)
    mix = mix + jnp.dot(d_out, wout_ref[SCONV_CH:, :], preferred_element_type=F32)
    o_ref[...] = h + mix


def _l1_call(h2, g, win_bf, conv_k, ln_g, ln_b, sg_w, sg_bt, wout_bf, seq, tm):
    t, d = h2.shape
    row = lambda i: (i, 0)
    return pl.pallas_call(
        functools.partial(_l1_kernel, blocks_per_seq=seq // tm),
        grid=(t // tm,),
        in_specs=[
            pl.BlockSpec((tm, d), row),
            _const_spec((1, d)),
            _const_spec(win_bf.shape),
            _const_spec(conv_k.shape),
            _const_spec((1, SG_CH)),
            _const_spec((1, SG_CH)),
            _const_spec(sg_w.shape),
            _const_spec(sg_bt.shape),
            _const_spec(wout_bf.shape),
        ],
        out_specs=pl.BlockSpec((tm, d), row),
        out_shape=jax.ShapeDtypeStruct((t, d), F32),
        scratch_shapes=[pltpu.VMEM((SCONV_HALO + tm, SCONV_CH), F32)],
        compiler_params=_params(("arbitrary",)),
        name="l1_mixer",
    )(h2, g, win_bf, conv_k, ln_g, ln_b, sg_w, sg_bt, wout_bf)


def _rope_tables(seq):
    half = HEAD_DIM // 2
    inv = ROPE_THETA ** (-jnp.arange(half, dtype=F32) / half)
    ang = jnp.arange(seq, dtype=F32)[:, None] * inv[None, :]
    cos, sin = jnp.cos(ang), jnp.sin(ang)
    reps = LANES // HEAD_DIM
    cos_t = jnp.tile(jnp.concatenate([cos, cos], axis=-1), (1, reps))
    sin_t = jnp.tile(jnp.concatenate([-sin, sin], axis=-1), (1, reps))
    return cos_t, sin_t


def kernel(x, norm_mix_g, norm_ffn_g, even_w_in, even_conv_k, even_conv_b, even_ln_g, even_ln_b,
           even_w_out, odd_w_in, odd_conv_k, odd_ln_g, odd_ln_b, odd_sg_w, odd_sg_b, odd_w_out,
           ffn_w1, ffn_w2, final_g):
    b, s, d = x.shape
    t = b * s
    x2 = x.reshape(t, d)
    cos_t, sin_t = _rope_tables(s)

    glu, q, k, v = _in0_call(x2, norm_mix_g[0:1], even_w_in[0].astype(BF16), cos_t, sin_t, s, 256)
    a = _conv_call(glu.reshape(b, s, CONV_CH), even_conv_k[0], even_conv_b[0:1],
                   even_ln_g[0:1], even_ln_b[0:1], 512)
    att = _attn_call(q.reshape(b, s, ATTN_WIDTH), k.reshape(b, s, ATTN_WIDTH),
                     v.reshape(b, s, ATTN_WIDTH))
    h = _out0_call(x2, a.reshape(t, CONV_CH), att.reshape(t, GROUP_WIDTH),
                   even_w_out[0].astype(BF16), 512)
    h = _ffn_call(h, norm_ffn_g[0:1], ffn_w1[0].astype(BF16), ffn_w2[0].astype(BF16),
                  final_g[None, :], False, 512)
    h = _l1_call(h, norm_mix_g[1:2], odd_w_in[0].astype(BF16), odd_conv_k[0], odd_ln_g[0:1],
                 odd_ln_b[0:1], odd_sg_w[0], odd_sg_b[0].T, odd_w_out[0].astype(BF16), s, 512)
    h = _ffn_call(h, norm_ffn_g[1:2], ffn_w1[1].astype(BF16), ffn_w2[1].astype(BF16),
                  final_g[None, :], True, 512)
    return h.reshape(b, s, d)
```

```python
import functools
import math

import jax
import jax.numpy as jnp
from jax import lax
from jax.experimental import pallas as pl
from jax.experimental.pallas import tpu as pltpu

F32 = jnp.float32
BF16 = jnp.bfloat16

EPS = 1e-6
LANES = 128
CONV_CH = 512
CONV_WIDTH = 31
HEAD_DIM = 64
HEADS_PER_GROUP = 8
DILATED_PAIRS = ((128, 1), (512, 4), (2048, 16))
N_GROUPS = len(DILATED_PAIRS)
GROUP_WIDTH = HEADS_PER_GROUP * HEAD_DIM
ATTN_WIDTH = N_GROUPS * GROUP_WIDTH
ROPE_THETA = 10000.0
SCONV_CH = 512
SCONV_WIDTH = 3
SG_GROUPS = 4
SG_HEAD = 128
SG_CH = SG_GROUPS * SG_HEAD
CHUNK = 128
BAND = 128
NEG_BIG = -1e30

VMEM_LIMIT = 56 * 1024 * 1024


def _rms(x, g):
    return x * lax.rsqrt(jnp.mean(x * x, axis=-1, keepdims=True) + EPS) * g


def _layer_norm(x, g, b):
    mu = jnp.mean(x, axis=-1, keepdims=True)
    xc = x - mu
    return xc * lax.rsqrt(jnp.mean(xc * xc, axis=-1, keepdims=True) + EPS) * g + b


def _const_spec(shape):
    return pl.BlockSpec(shape, lambda *_: (0,) * len(shape))


def _params(sem):
    return pltpu.CompilerParams(dimension_semantics=sem, vmem_limit_bytes=VMEM_LIMIT)


def _in0_kernel(x_ref, g_ref, w_ref, cos_ref, sin_ref, glu_ref, q_ref, k_ref, v_ref):
    tm = x_ref.shape[0]
    hn = _rms(x_ref[...], g_ref[...]).astype(BF16)

    def proj(c0, n):
        return jnp.dot(hn, w_ref[:, c0:c0 + n], preferred_element_type=F32)

    lin = proj(0, CONV_CH)
    gate = proj(CONV_CH, CONV_CH)
    glu_ref[...] = lin * jax.nn.sigmoid(gate)

    cos = cos_ref[...]
    sin = sin_ref[...]
    lane = lax.broadcasted_iota(jnp.int32, (tm, LANES), 1)
    first_half = (lane % HEAD_DIM) < (HEAD_DIM // 2)

    def rope(z):
        rot = jnp.where(first_half,
                        pltpu.roll(z, LANES - HEAD_DIM // 2, 1),
                        pltpu.roll(z, HEAD_DIM // 2, 1))
        return z * cos + rot * sin

    base = 2 * CONV_CH
    for c in range(N_GROUPS):
        zq = proj(base + c * GROUP_WIDTH, GROUP_WIDTH)
        zk = proj(base + ATTN_WIDTH + c * GROUP_WIDTH, GROUP_WIDTH)
        for s in range(GROUP_WIDTH // LANES):
            lo = c * GROUP_WIDTH + s * LANES
            q_ref[:, lo:lo + LANES] = rope(zq[:, s * LANES:(s + 1) * LANES]) * (HEAD_DIM ** -0.5)
            k_ref[:, lo:lo + LANES] = rope(zk[:, s * LANES:(s + 1) * LANES])
        v_ref[:, c * GROUP_WIDTH:(c + 1) * GROUP_WIDTH] = proj(
            base + 2 * ATTN_WIDTH + c * GROUP_WIDTH, GROUP_WIDTH)


def _in0_call(x2, g, w_bf, cos_t, sin_t, seq, tm):
    t, d = x2.shape
    n_in = w_bf.shape[1]
    blocks_per_seq = seq // tm
    row = lambda i: (i, 0)
    return pl.pallas_call(
        _in0_kernel,
        grid=(t // tm,),
        in_specs=[
            pl.BlockSpec((tm, d), row),
            _const_spec((1, d)),
            _const_spec((d, n_in)),
            pl.BlockSpec((tm, LANES), lambda i: (i % blocks_per_seq, 0)),
            pl.BlockSpec((tm, LANES), lambda i: (i % blocks_per_seq, 0)),
        ],
        out_specs=[
            pl.BlockSpec((tm, CONV_CH), row),
            pl.BlockSpec((tm, ATTN_WIDTH), row),
            pl.BlockSpec((tm, ATTN_WIDTH), row),
            pl.BlockSpec((tm, ATTN_WIDTH), row),
        ],
        out_shape=[
            jax.ShapeDtypeStruct((t, CONV_CH), F32),
            jax.ShapeDtypeStruct((t, ATTN_WIDTH), F32),
            jax.ShapeDtypeStruct((t, ATTN_WIDTH), F32),
            jax.ShapeDtypeStruct((t, ATTN_WIDTH), F32),
        ],
        compiler_params=_params(("parallel",)),
        name="l0_in_proj",
    )(x2, g, w_bf, cos_t, sin_t)


CONV_HALO = 32
CONV_ROWS = 64


def _conv_kernel(prev_ref, cur_ref, k_ref, b_ref, lg_ref, lb_ref, out_ref, buf):
    tr = cur_ref.shape[0]
    i = pl.program_id(1)
    buf[0:CONV_HALO, :] = jnp.where(i > 0, prev_ref[...], 0.0)
    buf[CONV_HALO:CONV_HALO + tr, :] = cur_ref[...]
    bias = b_ref[...]
    lg = lg_ref[...]
    lb = lb_ref[...]
    off = CONV_HALO - (CONV_WIDTH - 1)
    for r0 in range(0, tr, CONV_ROWS):
        acc = jnp.zeros((CONV_ROWS, CONV_CH), F32)
        for j in range(CONV_WIDTH):
            acc = acc + buf[r0 + off + j:r0 + off + j + CONV_ROWS, :] * k_ref[j:j + 1, :]
        y = _layer_norm(acc + bias, lg, lb)
        out_ref[r0:r0 + CONV_ROWS, :] = y * jax.nn.sigmoid(y)


def _conv_call(glu3, conv_k, conv_b, ln_g, ln_b, tr):
    b, s, c = glu3.shape
    halo_blocks = tr // CONV_HALO
    return pl.pallas_call(
        _conv_kernel,
        grid=(b, s // tr),
        in_specs=[
            pl.BlockSpec((None, CONV_HALO, c),
                         lambda bi, i: (bi, jnp.maximum(i * halo_blocks - 1, 0), 0)),
            pl.BlockSpec((None, tr, c), lambda bi, i: (bi, i, 0)),
            _const_spec((CONV_WIDTH, c)),
            _const_spec((1, c)),
            _const_spec((1, c)),
            _const_spec((1, c)),
        ],
        out_specs=pl.BlockSpec((None, tr, c), lambda bi, i: (bi, i, 0)),
        out_shape=jax.ShapeDtypeStruct((b, s, c), F32),
        scratch_shapes=[pltpu.VMEM((CONV_HALO + tr, c), F32)],
        compiler_params=_params(("parallel", "parallel")),
        name="l0_conv_ln_swish",
    )(glu3, glu3, conv_k, conv_b, ln_g, ln_b)


ATTN_UNROLL = 4


def _attn_kernel(q0, k0, v0, q1, k1, v1, q2, k2, v2, out_ref, acc_o, acc_m, acc_l):
    seq = out_ref.shape[0]
    lane = lax.broadcasted_iota(jnp.int32, (BAND, LANES), 1)
    head0 = lane < HEAD_DIM
    row2 = lax.broadcasted_iota(jnp.int32, (2 * BAND, 2 * BAND), 0) & (BAND - 1)
    col2 = lax.broadcasted_iota(jnp.int32, (2 * BAND, 2 * BAND), 1)
    allowed = jnp.logical_or(jnp.logical_and(col2 < BAND, col2 >= row2),
                             jnp.logical_and(col2 >= BAND, (col2 - BAND) <= row2))
    bias_pc = jnp.where(allowed, 0.0, NEG_BIG).astype(F32)
    bias_noprev = jnp.where(col2 < BAND, NEG_BIG, 0.0).astype(F32)
    row1 = lax.broadcasted_iota(jnp.int32, (2 * BAND, BAND), 0) & (BAND - 1)
    col1 = lax.broadcasted_iota(jnp.int32, (2 * BAND, BAND), 1)
    bias_c = jnp.where(col1 <= row1, 0.0, NEG_BIG).astype(F32)

    def attend(q, kb, vb, bias):
        qs = jnp.concatenate([jnp.where(head0, q, 0.0), jnp.where(head0, 0.0, q)],
                             axis=0).astype(BF16)
        s = lax.dot_general(qs, kb, (((1,), (1,)), ((), ())), preferred_element_type=F32) + bias
        m = jnp.max(s, axis=-1, keepdims=True)
        p = jnp.exp(s - m)
        l = jnp.sum(p, axis=-1, keepdims=True)
        o = jnp.dot(p.astype(BF16), vb, preferred_element_type=F32)
        o = jnp.where(head0, o[:BAND], o[BAND:])
        m = jnp.where(head0, m[:BAND], m[BAND:])
        l = jnp.where(head0, l[:BAND], l[BAND:])
        return o, m, l

    groups = ((q0, k0, v0), (q1, k1, v1), (q2, k2, v2))
    order = sorted(range(N_GROUPS), key=lambda g: -DILATED_PAIRS[g][1])
    for pos, g in enumerate(order):
        dil = DILATED_PAIRS[g][1]
        q_ref, k_ref, v_ref = groups[g]
        nb = (seq // dil) // BAND
        assert nb == 1 or nb % ATTN_UNROLL == 0
        span = dil * BAND
        first, last = pos == 0, pos == N_GROUPS - 1
        runs = max(nb // ATTN_UNROLL, 1)

        def rows(start, dil=dil):
            return pl.ds(start, BAND) if dil == 1 else pl.ds(start, BAND, stride=dil)

        def emit(idx, o, m, l, first=first, last=last):
            if first:
                acc_o[idx, :] = o
                acc_m[idx, :] = m
                acc_l[idx, :] = l
                return
            m_old = acc_m[idx, :]
            m_new = jnp.maximum(m_old, m)
            a_old = jnp.exp(m_old - m_new)
            a_new = jnp.exp(m - m_new)
            o_new = acc_o[idx, :] * a_old + o * a_new
            l_new = acc_l[idx, :] * a_old + l * a_new
            if last:
                out_ref[idx, :] = o_new / l_new
            else:
                acc_o[idx, :] = o_new
                acc_l[idx, :] = l_new
                acc_m[idx, :] = m_new

        def body(it, carry, nb=nb, span=span, runs=runs, rows=rows, emit=emit,
                 q_ref=q_ref, k_ref=k_ref, v_ref=v_ref):
            if nb == 1:
                for u in range(ATTN_UNROLL):
                    idx = rows(it * ATTN_UNROLL + u)
                    o, m, l = attend(q_ref[idx, :], k_ref[idx, :].astype(BF16),
                                     v_ref[idx, :].astype(BF16), bias_c)
                    emit(idx, o, m, l)
                return carry
            r = it // runs
            n0 = (it % runs) * ATTN_UNROLL
            base = r + n0 * span
            has_prev = runs > 1
            kbs, vbs = [], []
            for j in range(-1 if has_prev else 0, ATTN_UNROLL):
                idx = rows(jnp.maximum(base + j * span, 0) if j < 0 else base + j * span)
                kbs.append(k_ref[idx, :].astype(BF16))
                vbs.append(v_ref[idx, :].astype(BF16))
            off = 1 if has_prev else 0
            for u in range(ATTN_UNROLL):
                idx = rows(base + u * span)
                if u == 0 and not has_prev:
                    o, m, l = attend(q_ref[idx, :], kbs[0], vbs[0], bias_c)
                else:
                    bias = bias_pc
                    if u == 0:
                        bias = bias + jnp.where(n0 > 0, 0.0, 1.0) * bias_noprev
                    o, m, l = attend(q_ref[idx, :],
                                     jnp.concatenate([kbs[u + off - 1], kbs[u + off]], axis=0),
                                     jnp.concatenate([vbs[u + off - 1], vbs[u + off]], axis=0), bias)
                emit(idx, o, m, l)
            return carry

        n_iter = dil // ATTN_UNROLL if nb == 1 else dil * runs
        lax.fori_loop(0, n_iter, body, 0)


def _attn_call(q3, k3, v3):
    b, s, _ = q3.shape
    pairs = GROUP_WIDTH // LANES
    in_specs = []
    args = []
    for g in range(N_GROUPS):
        for arr in (q3, k3, v3):
            in_specs.append(pl.BlockSpec((None, s, LANES),
                                         lambda bi, p, g=g: (bi, 0, g * pairs + p)))
            args.append(arr)
    return pl.pallas_call(
        _attn_kernel,
        grid=(b, pairs),
        in_specs=in_specs,
        out_specs=pl.BlockSpec((None, s, LANES), lambda bi, p: (bi, 0, p)),
        out_shape=jax.ShapeDtypeStruct((b, s, GROUP_WIDTH), F32),
        scratch_shapes=[pltpu.VMEM((s, LANES), F32)] * 3,
        compiler_params=_params(("parallel", "parallel")),
        name="l0_dilated_attention",
    )(*args)


def _out0_kernel(x_ref, a_ref, att_ref, w_ref, o_ref):
    c = a_ref.shape[1]
    mix = jnp.dot(a_ref[...].astype(BF16), w_ref[0:c, :], preferred_element_type=F32)
    mix = mix + jnp.dot(att_ref[...].astype(BF16), w_ref[c:, :], preferred_element_type=F32)
    o_ref[...] = x_ref[...] + mix


def _out0_call(x2, a2, att2, w_bf, tm):
    t, d = x2.shape
    row = lambda i: (i, 0)
    return pl.pallas_call(
        _out0_kernel,
        grid=(t // tm,),
        in_specs=[
            pl.BlockSpec((tm, d), row),
            pl.BlockSpec((tm, a2.shape[1]), row),
            pl.BlockSpec((tm, att2.shape[1]), row),
            _const_spec(w_bf.shape),
        ],
        out_specs=pl.BlockSpec((tm, d), row),
        out_shape=jax.ShapeDtypeStruct((t, d), F32),
        compiler_params=_params(("parallel",)),
        name="l0_out_proj",
    )(x2, a2, att2, w_bf)


FFN_CHUNK = 1024


def _ffn_kernel(h_ref, g_ref, w1_ref, w2_ref, fg_ref, o_ref, *, final_norm):
    h = h_ref[...]
    hn = _rms(h, g_ref[...]).astype(BF16)
    d_ff = w1_ref.shape[1]
    acc = h
    for c0 in range(0, d_ff, FFN_CHUNK):
        a = jnp.dot(hn, w1_ref[:, c0:c0 + FFN_CHUNK], preferred_element_type=F32)
        a = jnp.square(jnp.maximum(a, 0.0)).astype(BF16)
        acc = acc + jnp.dot(a, w2_ref[c0:c0 + FFN_CHUNK, :], preferred_element_type=F32)
    if final_norm:
        acc = _rms(acc, fg_ref[...])
    o_ref[...] = acc


def _ffn_call(h2, g, w1_bf, w2_bf, final_g, final_norm, tm):
    t, d = h2.shape
    row = lambda i: (i, 0)
    return pl.pallas_call(
        functools.partial(_ffn_kernel, final_norm=final_norm),
        grid=(t // tm,),
        in_specs=[
            pl.BlockSpec((tm, d), row),
            _const_spec((1, d)),
            _const_spec(w1_bf.shape),
            _const_spec(w2_bf.shape),
            _const_spec((1, d)),
        ],
        out_specs=pl.BlockSpec((tm, d), row),
        out_shape=jax.ShapeDtypeStruct((t, d), F32),
        compiler_params=_params(("parallel",)),
        name="ffn_final" if final_norm else "ffn",
    )(h2, g, w1_bf, w2_bf, final_g)


SCONV_HALO = 8
GELU_C = math.sqrt(2.0 / math.pi)


def _gelu_tanh(x):
    return 0.5 * x * (1.0 + jnp.tanh(GELU_C * (x + 0.044715 * (x * x * x))))


def _l1_kernel(h_ref, g_ref, win_ref, ck_ref, lg_ref, lb_ref, sgw_ref, sgbt_ref, wout_ref,
               o_ref, ybuf, *, blocks_per_seq):
    tm = h_ref.shape[0]
    i = pl.program_id(0)
    h = h_ref[...]
    hn = _rms(h, g_ref[...]).astype(BF16)

    def proj(c0, n):
        return jnp.dot(hn, win_ref[:, c0:c0 + n], preferred_element_type=F32)

    gb = proj(0, SCONV_CH)
    y = proj(SCONV_CH, SCONV_CH) * proj(2 * SCONV_CH, SCONV_CH)

    @pl.when(i % blocks_per_seq == 0)
    def _():
        ybuf[0:SCONV_HALO, :] = jnp.zeros((SCONV_HALO, SCONV_CH), F32)

    ybuf[SCONV_HALO:SCONV_HALO + tm, :] = y
    conv = y * ck_ref[SCONV_WIDTH - 1:SCONV_WIDTH, :]
    for j in range(SCONV_WIDTH - 1):
        sh = SCONV_WIDTH - 1 - j
        conv = conv + ybuf[SCONV_HALO - sh:SCONV_HALO - sh + tm, :] * ck_ref[j:j + 1, :]
    ybuf[0:SCONV_HALO, :] = y[tm - SCONV_HALO:tm, :]
    c_out = (gb * conv).astype(BF16)

    u = _gelu_tanh(proj(3 * SCONV_CH, SG_CH))
    v = _gelu_tanh(proj(3 * SCONV_CH + SG_CH, SG_CH))
    v = _layer_norm(v, lg_ref[...], lb_ref[...]).astype(BF16)
    rr = lax.broadcasted_iota(jnp.int32, (CHUNK, CHUNK), 0)
    cc = lax.broadcasted_iota(jnp.int32, (CHUNK, CHUNK), 1)
    causal = rr >= cc
    d_cols = []
    for gi in range(SG_GROUPS):
        ws = jnp.where(causal, sgw_ref[gi], 0.0).astype(BF16)
        bcol = sgbt_ref[:, gi:gi + 1]
        chunks = []
        for n in range(tm // CHUNK):
            vv = v[n * CHUNK:(n + 1) * CHUNK, gi * SG_HEAD:(gi + 1) * SG_HEAD]
            chunks.append(jnp.dot(ws, vv, preferred_element_type=F32) + bcol)
        d_cols.append(jnp.concatenate(chunks, axis=0))
    d_out = (u * jnp.concatenate(d_cols, axis=1)).astype(BF16)

    mix = jnp.dot(c_out, wout_ref[0:SCONV_CH, :], preferred_element_type=F32)
    mix = mix + jnp.dot(d_out, wout_ref[SCONV_CH:, :], preferred_element_type=F32)
    o_ref[...] = h + mix


def _l1_call(h2, g, win_bf, conv_k, ln_g, ln_b, sg_w, sg_bt, wout_bf, seq, tm):
    t, d = h2.shape
    row = lambda i: (i, 0)
    return pl.pallas_call(
        functools.partial(_l1_kernel, blocks_per_seq=seq // tm),
        grid=(t // tm,),
        in_specs=[
            pl.BlockSpec((tm, d), row),
            _const_spec((1, d)),
            _const_spec(win_bf.shape),
            _const_spec(conv_k.shape),
            _const_spec((1, SG_CH)),
            _const_spec((1, SG_CH)),
            _const_spec(sg_w.shape),
            _const_spec(sg_bt.shape),
            _const_spec(wout_bf.shape),
        ],
        out_specs=pl.BlockSpec((tm, d), row),
        out_shape=jax.ShapeDtypeStruct((t, d), F32),
        scratch_shapes=[pltpu.VMEM((SCONV_HALO + tm, SCONV_CH), F32)],
        compiler_params=_params(("arbitrary",)),
        name="l1_mixer",
    )(h2, g, win_bf, conv_k, ln_g, ln_b, sg_w, sg_bt, wout_bf)


def _rope_tables(seq):
    half = HEAD_DIM // 2
    inv = ROPE_THETA ** (-jnp.arange(half, dtype=F32) / half)
    ang = jnp.arange(seq, dtype=F32)[:, None] * inv[None, :]
    cos, sin = jnp.cos(ang), jnp.sin(ang)
    reps = LANES // HEAD_DIM
    cos_t = jnp.tile(jnp.concatenate([cos, cos], axis=-1), (1, reps))
    sin_t = jnp.tile(jnp.concatenate([-sin, sin], axis=-1), (1, reps))
    return cos_t, sin_t


def kernel(x, norm_mix_g, norm_ffn_g, even_w_in, even_conv_k, even_conv_b, even_ln_g, even_ln_b,
           even_w_out, odd_w_in, odd_conv_k, odd_ln_g, odd_ln_b, odd_sg_w, odd_sg_b, odd_w_out,
           ffn_w1, ffn_w2, final_g):
    b, s, d = x.shape
    t = b * s
    x2 = x.reshape(t, d)
    cos_t, sin_t = _rope_tables(s)

    glu, q, k, v = _in0_call(x2, norm_mix_g[0:1], even_w_in[0].astype(BF16), cos_t, sin_t, s, 256)
    a = _conv_call(glu.reshape(b, s, CONV_CH), even_conv_k[0], even_conv_b[0:1],
                   even_ln_g[0:1], even_ln_b[0:1], 512)
    att = _attn_call(q.reshape(b, s, ATTN_WIDTH), k.reshape(b, s, ATTN_WIDTH),
                     v.reshape(b, s, ATTN_WIDTH))
    h = _out0_call(x2, a.reshape(t, CONV_CH), att.reshape(t, GROUP_WIDTH),
                   even_w_out[0].astype(BF16), 512)
    h = _ffn_call(h, norm_ffn_g[0:1], ffn_w1[0].astype(BF16), ffn_w2[0].astype(BF16),
                  final_g[None, :], False, 512)
    h = _l1_call(h, norm_mix_g[1:2], odd_w_in[0].astype(BF16), odd_conv_k[0], odd_ln_g[0:1],
                 odd_ln_b[0:1], odd_sg_w[0], odd_sg_b[0].T, odd_w_out[0].astype(BF16), s, 512)
    h = _ffn_call(h, norm_ffn_g[1:2], ffn_w1[1].astype(BF16), ffn_w2[1].astype(BF16),
                  final_g[None, :], True, 512)
    return h.reshape(b, s, d)
```

```python
import functools
import math

import jax
import jax.numpy as jnp
from jax import lax
from jax.experimental import pallas as pl
from jax.experimental.pallas import tpu as pltpu

F32 = jnp.float32
BF16 = jnp.bfloat16

EPS = 1e-6
LANES = 128
CONV_CH = 512
CONV_WIDTH = 31
HEAD_DIM = 64
HEADS_PER_GROUP = 8
DILATED_PAIRS = ((128, 1), (512, 4), (2048, 16))
N_GROUPS = len(DILATED_PAIRS)
GROUP_WIDTH = HEADS_PER_GROUP * HEAD_DIM
ATTN_WIDTH = N_GROUPS * GROUP_WIDTH
ROPE_THETA = 10000.0
SCONV_CH = 512
SCONV_WIDTH = 3
SG_GROUPS = 4
SG_HEAD = 128
SG_CH = SG_GROUPS * SG_HEAD
CHUNK = 128
BAND = 128
NEG_BIG = -1e30

VMEM_LIMIT = 56 * 1024 * 1024


def _rms(x, g):
    return x * lax.rsqrt(jnp.mean(x * x, axis=-1, keepdims=True) + EPS) * g


def _layer_norm(x, g, b):
    mu = jnp.mean(x, axis=-1, keepdims=True)
    xc = x - mu
    return xc * lax.rsqrt(jnp.mean(xc * xc, axis=-1, keepdims=True) + EPS) * g + b


def _const_spec(shape):
    return pl.BlockSpec(shape, lambda *_: (0,) * len(shape))


def _params(sem):
    return pltpu.CompilerParams(dimension_semantics=sem, vmem_limit_bytes=VMEM_LIMIT)


SUBLANES = 8
CONV_HALO = 32
CONV_ROWS = 64
CONV_OFF = CONV_HALO - (CONV_WIDTH - 1)


def _in0_kernel(x_ref, g_ref, w_ref, cos_ref, sin_ref, ck_ref, cb_ref, lg_ref, lb_ref,
                a_ref, q_ref, k_ref, v_ref, gbuf, phase, *, blocks_per_seq):
    tm = x_ref.shape[0]
    i = pl.program_id(0)
    hn = _rms(x_ref[...], g_ref[...]).astype(BF16)

    def proj(c0, n):
        return jnp.dot(hn, w_ref[:, c0:c0 + n], preferred_element_type=F32)

    @pl.when(i % blocks_per_seq == 0)
    def _():
        gbuf[0:CONV_HALO, :] = jnp.zeros((CONV_HALO, CONV_CH), F32)

    gbuf[CONV_HALO:CONV_HALO + tm, :] = proj(0, CONV_CH) * jax.nn.sigmoid(proj(CONV_CH, CONV_CH))
    span = tm + CONV_HALO - SUBLANES
    for b in range(1, SUBLANES):
        phase[b - 1, 0:span, :] = gbuf[b:b + span, :]
    bias = cb_ref[...]
    lg = lg_ref[...]
    lb = lb_ref[...]

    def conv_block(r0):
        acc = jnp.zeros((CONV_ROWS, CONV_CH), F32)
        for j in range(CONV_WIDTH):
            a8, b = divmod(j + CONV_OFF, SUBLANES)
            lo = r0 + a8 * SUBLANES
            if b == 0:
                tap = gbuf[lo:lo + CONV_ROWS, :]
            else:
                tap = phase[b - 1, lo:lo + CONV_ROWS, :]
            acc = acc + tap * ck_ref[j:j + 1, :]
        y = _layer_norm(acc + bias, lg, lb)
        a_ref[r0:r0 + CONV_ROWS, :] = (y * jax.nn.sigmoid(y)).astype(a_ref.dtype)

    cos = cos_ref[...]
    sin = sin_ref[...]
    lane = lax.broadcasted_iota(jnp.int32, (tm, LANES), 1)
    first_half = (lane % HEAD_DIM) < (HEAD_DIM // 2)

    def rope(z):
        rot = jnp.where(first_half,
                        pltpu.roll(z, LANES - HEAD_DIM // 2, 1),
                        pltpu.roll(z, HEAD_DIM // 2, 1))
        return z * cos + rot * sin

    base = 2 * CONV_CH

    def proj_group(c):
        zq = proj(base + c * GROUP_WIDTH, GROUP_WIDTH)
        zk = proj(base + ATTN_WIDTH + c * GROUP_WIDTH, GROUP_WIDTH)
        for s in range(GROUP_WIDTH // LANES):
            lo = c * GROUP_WIDTH + s * LANES
            q_ref[:, lo:lo + LANES] = rope(zq[:, s * LANES:(s + 1) * LANES]) * (HEAD_DIM ** -0.5)
            k_ref[:, lo:lo + LANES] = rope(zk[:, s * LANES:(s + 1) * LANES])
        v_ref[:, c * GROUP_WIDTH:(c + 1) * GROUP_WIDTH] = proj(
            base + 2 * ATTN_WIDTH + c * GROUP_WIDTH, GROUP_WIDTH)

    conv_starts = list(range(0, tm, CONV_ROWS))
    per_group = -(-len(conv_starts) // N_GROUPS)
    for c in range(N_GROUPS):
        proj_group(c)
        for r0 in conv_starts[c * per_group:(c + 1) * per_group]:
            conv_block(r0)
    gbuf[0:CONV_HALO, :] = gbuf[tm:tm + CONV_HALO, :]


def _in0_call(x2, g, w_bf, cos_t, sin_t, conv_k, conv_b, ln_g, ln_b, seq, tm):
    t, d = x2.shape
    n_in = w_bf.shape[1]
    blocks_per_seq = seq // tm
    row = lambda i: (i, 0)
    return pl.pallas_call(
        functools.partial(_in0_kernel, blocks_per_seq=blocks_per_seq),
        grid=(t // tm,),
        in_specs=[
            pl.BlockSpec((tm, d), row),
            _const_spec((1, d)),
            _const_spec((d, n_in)),
            pl.BlockSpec((tm, LANES), lambda i: (i % blocks_per_seq, 0)),
            pl.BlockSpec((tm, LANES), lambda i: (i % blocks_per_seq, 0)),
            _const_spec((CONV_WIDTH, CONV_CH)),
            _const_spec((1, CONV_CH)),
            _const_spec((1, CONV_CH)),
            _const_spec((1, CONV_CH)),
        ],
        out_specs=[
            pl.BlockSpec((tm, CONV_CH), row),
            pl.BlockSpec((tm, ATTN_WIDTH), row),
            pl.BlockSpec((tm, ATTN_WIDTH), row),
            pl.BlockSpec((tm, ATTN_WIDTH), row),
        ],
        out_shape=[
            jax.ShapeDtypeStruct((t, CONV_CH), BF16),
            jax.ShapeDtypeStruct((t, ATTN_WIDTH), F32),
            jax.ShapeDtypeStruct((t, ATTN_WIDTH), F32),
            jax.ShapeDtypeStruct((t, ATTN_WIDTH), F32),
        ],
        scratch_shapes=[
            pltpu.VMEM((CONV_HALO + tm, CONV_CH), F32),
            pltpu.VMEM((SUBLANES - 1, CONV_HALO + tm - SUBLANES, CONV_CH), F32),
        ],
        compiler_params=_params(("arbitrary",)),
        name="l0_in_proj",
    )(x2, g, w_bf, cos_t, sin_t, conv_k, conv_b, ln_g, ln_b)


ATTN_UNROLL = 4


def _attn_kernel(q0, k0, v0, q1, k1, v1, q2, k2, v2, out_ref, acc_o, acc_m, acc_l):
    seq = out_ref.shape[0]
    lane = lax.broadcasted_iota(jnp.int32, (BAND, LANES), 1)
    head0 = lane < HEAD_DIM
    row2 = lax.broadcasted_iota(jnp.int32, (2 * BAND, 2 * BAND), 0) & (BAND - 1)
    col2 = lax.broadcasted_iota(jnp.int32, (2 * BAND, 2 * BAND), 1)
    allowed = jnp.logical_or(jnp.logical_and(col2 < BAND, col2 >= row2),
                             jnp.logical_and(col2 >= BAND, (col2 - BAND) <= row2))
    bias_pc = jnp.where(allowed, 0.0, NEG_BIG).astype(F32)
    bias_noprev = jnp.where(col2 < BAND, NEG_BIG, 0.0).astype(F32)
    row1 = lax.broadcasted_iota(jnp.int32, (2 * BAND, BAND), 0) & (BAND - 1)
    col1 = lax.broadcasted_iota(jnp.int32, (2 * BAND, BAND), 1)
    bias_c = jnp.where(col1 <= row1, 0.0, NEG_BIG).astype(F32)

    def attend(q, kb, vb, bias):
        qs = jnp.concatenate([jnp.where(head0, q, 0.0), jnp.where(head0, 0.0, q)],
                             axis=0).astype(BF16)
        s = lax.dot_general(qs, kb, (((1,), (1,)), ((), ())), preferred_element_type=F32) + bias
        m = jnp.max(s, axis=-1, keepdims=True)
        p = jnp.exp(s - m)
        l = jnp.sum(p, axis=-1, keepdims=True)
        o = jnp.dot(p.astype(BF16), vb, preferred_element_type=F32)
        o = jnp.where(head0, o[:BAND], o[BAND:])
        m = jnp.where(head0, m[:BAND], m[BAND:])
        l = jnp.where(head0, l[:BAND], l[BAND:])
        return o, m, l

    groups = ((q0, k0, v0), (q1, k1, v1), (q2, k2, v2))
    order = sorted(range(N_GROUPS), key=lambda g: -DILATED_PAIRS[g][1])
    for pos, g in enumerate(order):
        dil = DILATED_PAIRS[g][1]
        q_ref, k_ref, v_ref = groups[g]
        nb = (seq // dil) // BAND
        assert nb == 1 or nb % ATTN_UNROLL == 0
        span = dil * BAND
        first, last = pos == 0, pos == N_GROUPS - 1
        runs = max(nb // ATTN_UNROLL, 1)

        def rows(start, dil=dil):
            return pl.ds(start, BAND) if dil == 1 else pl.ds(start, BAND, stride=dil)

        def emit(idx, o, m, l, first=first, last=last):
            if first:
                acc_o[idx, :] = o
                acc_m[idx, :] = m
                acc_l[idx, :] = l
                return
            m_old = acc_m[idx, :]
            m_new = jnp.maximum(m_old, m)
            a_old = jnp.exp(m_old - m_new)
            a_new = jnp.exp(m - m_new)
            o_new = acc_o[idx, :] * a_old + o * a_new
            l_new = acc_l[idx, :] * a_old + l * a_new
            if last:
                out_ref[idx, :] = o_new / l_new
            else:
                acc_o[idx, :] = o_new
                acc_l[idx, :] = l_new
                acc_m[idx, :] = m_new

        def body(it, carry, nb=nb, span=span, runs=runs, rows=rows, emit=emit,
                 q_ref=q_ref, k_ref=k_ref, v_ref=v_ref):
            if nb == 1:
                for u in range(ATTN_UNROLL):
                    idx = rows(it * ATTN_UNROLL + u)
                    o, m, l = attend(q_ref[idx, :], k_ref[idx, :].astype(BF16),
                                     v_ref[idx, :].astype(BF16), bias_c)
                    emit(idx, o, m, l)
                return carry
            r = it // runs
            n0 = (it % runs) * ATTN_UNROLL
            base = r + n0 * span
            has_prev = runs > 1
            kbs, vbs = [], []
            for j in range(-1 if has_prev else 0, ATTN_UNROLL):
                idx = rows(jnp.maximum(base + j * span, 0) if j < 0 else base + j * span)
                kbs.append(k_ref[idx, :].astype(BF16))
                vbs.append(v_ref[idx, :].astype(BF16))
            off = 1 if has_prev else 0
            for u in range(ATTN_UNROLL):
                idx = rows(base + u * span)
                if u == 0 and not has_prev:
                    o, m, l = attend(q_ref[idx, :], kbs[0], vbs[0], bias_c)
                else:
                    bias = bias_pc
                    if u == 0:
                        bias = bias + jnp.where(n0 > 0, 0.0, 1.0) * bias_noprev
                    o, m, l = attend(q_ref[idx, :],
                                     jnp.concatenate([kbs[u + off - 1], kbs[u + off]], axis=0),
                                     jnp.concatenate([vbs[u + off - 1], vbs[u + off]], axis=0), bias)
                emit(idx, o, m, l)
            return carry

        n_iter = dil // ATTN_UNROLL if nb == 1 else dil * runs
        lax.fori_loop(0, n_iter, body, 0)


def _attn_call(q3, k3, v3):
    b, s, _ = q3.shape
    pairs = GROUP_WIDTH // LANES
    in_specs = []
    args = []
    for g in range(N_GROUPS):
        for arr in (q3, k3, v3):
            in_specs.append(pl.BlockSpec((None, s, LANES),
                                         lambda bi, p, g=g: (bi, 0, g * pairs + p)))
            args.append(arr)
    return pl.pallas_call(
        _attn_kernel,
        grid=(b, pairs),
        in_specs=in_specs,
        out_specs=pl.BlockSpec((None, s, LANES), lambda bi, p: (bi, 0, p)),
        out_shape=jax.ShapeDtypeStruct((b, s, GROUP_WIDTH), F32),
        scratch_shapes=[pltpu.VMEM((s, LANES), F32)] * 3,
        compiler_params=_params(("parallel", "parallel")),
        name="l0_dilated_attention",
    )(*args)


def _out0_kernel(x_ref, a_ref, att_ref, w_ref, o_ref):
    c = a_ref.shape[1]
    mix = jnp.dot(a_ref[...].astype(BF16), w_ref[0:c, :], preferred_element_type=F32)
    mix = mix + jnp.dot(att_ref[...].astype(BF16), w_ref[c:, :], preferred_element_type=F32)
    o_ref[...] = x_ref[...] + mix


def _out0_call(x2, a2, att2, w_bf, tm):
    t, d = x2.shape
    row = lambda i: (i, 0)
    return pl.pallas_call(
        _out0_kernel,
        grid=(t // tm,),
        in_specs=[
            pl.BlockSpec((tm, d), row),
            pl.BlockSpec((tm, a2.shape[1]), row),
            pl.BlockSpec((tm, att2.shape[1]), row),
            _const_spec(w_bf.shape),
        ],
        out_specs=pl.BlockSpec((tm, d), row),
        out_shape=jax.ShapeDtypeStruct((t, d), F32),
        compiler_params=_params(("parallel",)),
        name="l0_out_proj",
    )(x2, a2, att2, w_bf)


FFN_CHUNK = 1024


def _ffn_kernel(h_ref, g_ref, w1_ref, w2_ref, fg_ref, o_ref, *, final_norm):
    h = h_ref[...]
    hn = _rms(h, g_ref[...]).astype(BF16)
    d_ff = w1_ref.shape[1]
    acc = h
    for c0 in range(0, d_ff, FFN_CHUNK):
        a = jnp.dot(hn, w1_ref[:, c0:c0 + FFN_CHUNK], preferred_element_type=F32)
        a = jnp.square(jnp.maximum(a, 0.0)).astype(BF16)
        acc = acc + jnp.dot(a, w2_ref[c0:c0 + FFN_CHUNK, :], preferred_element_type=F32)
    if final_norm:
        acc = _rms(acc, fg_ref[...])
    o_ref[...] = acc


def _ffn_call(h2, g, w1_bf, w2_bf, final_g, final_norm, tm):
    t, d = h2.shape
    row = lambda i: (i, 0)
    return pl.pallas_call(
        functools.partial(_ffn_kernel, final_norm=final_norm),
        grid=(t // tm,),
        in_specs=[
            pl.BlockSpec((tm, d), row),
            _const_spec((1, d)),
            _const_spec(w1_bf.shape),
            _const_spec(w2_bf.shape),
            _const_spec((1, d)),
        ],
        out_specs=pl.BlockSpec((tm, d), row),
        out_shape=jax.ShapeDtypeStruct((t, d), F32),
        compiler_params=_params(("parallel",)),
        name="ffn_final" if final_norm else "ffn",
    )(h2, g, w1_bf, w2_bf, final_g)


SCONV_HALO = 8
GELU_C = math.sqrt(2.0 / math.pi)


def _gelu_tanh(x):
    return 0.5 * x * (1.0 + jnp.tanh(GELU_C * (x + 0.044715 * (x * x * x))))


def _l1_kernel(h_ref, g_ref, win_ref, ck_ref, lg_ref, lb_ref, sgw_ref, sgbt_ref, wout_ref,
               o_ref, ybuf, *, blocks_per_seq):
    tm = h_ref.shape[0]
    i = pl.program_id(0)
    h = h_ref[...]
    hn = _rms(h, g_ref[...]).astype(BF16)

    def proj(c0, n):
        return jnp.dot(hn, win_ref[:, c0:c0 + n], preferred_element_type=F32)

    gb = proj(0, SCONV_CH)
    y = proj(SCONV_CH, SCONV_CH) * proj(2 * SCONV_CH, SCONV_CH)

    @pl.when(i % blocks_per_seq == 0)
    def _():
        ybuf[0:SCONV_HALO, :] = jnp.zeros((SCONV_HALO, SCONV_CH), F32)

    ybuf[SCONV_HALO:SCONV_HALO + tm, :] = y
    conv = y * ck_ref[SCONV_WIDTH - 1:SCONV_WIDTH, :]
    for j in range(SCONV_WIDTH - 1):
        sh = SCONV_WIDTH - 1 - j
        conv = conv + ybuf[SCONV_HALO - sh:SCONV_HALO - sh + tm, :] * ck_ref[j:j + 1, :]
    ybuf[0:SCONV_HALO, :] = y[tm - SCONV_HALO:tm, :]
    c_out = (gb * conv).astype(BF16)

    u = _gelu_tanh(proj(3 * SCONV_CH, SG_CH))
    v = _gelu_tanh(proj(3 * SCONV_CH + SG_CH, SG_CH))
    v = _layer_norm(v, lg_ref[...], lb_ref[...]).astype(BF16)
    rr = lax.broadcasted_iota(jnp.int32, (CHUNK, CHUNK), 0)
    cc = lax.broadcasted_iota(jnp.int32, (CHUNK, CHUNK), 1)
    causal = rr >= cc
    d_cols = []
    for gi in range(SG_GROUPS):
        ws = jnp.where(causal, sgw_ref[gi], 0.0).astype(BF16)
        bcol = sgbt_ref[:, gi:gi + 1]
        chunks = []
        for n in range(tm // CHUNK):
            vv = v[n * CHUNK:(n + 1) * CHUNK, gi * SG_HEAD:(gi + 1) * SG_HEAD]
            chunks.append(jnp.dot(ws, vv, preferred_element_type=F32) + bcol)
        d_cols.append(jnp.concatenate(chunks, axis=0))
    d_out = (u * jnp.concatenate(d_cols, axis=1)).astype(BF16)

    mix = jnp.dot(c_out, wout_ref[0:SCONV_CH, :], preferred_element_type=F32)
    mix = mix + jnp.dot(d_out, wout_ref[SCONV_CH:, :], preferred_element_type=F32)
    o_ref[...] = h + mix


def _l1_call(h2, g, win_bf, conv_k, ln_g, ln_b, sg_w, sg_bt, wout_bf, seq, tm):
    t, d = h2.shape
    row = lambda i: (i, 0)
    return pl.pallas_call(
        functools.partial(_l1_kernel, blocks_per_seq=seq // tm),
        grid=(t // tm,),
        in_specs=[
            pl.BlockSpec((tm, d), row),
            _const_spec((1, d)),
            _const_spec(win_bf.shape),
            _const_spec(conv_k.shape),
            _const_spec((1, SG_CH)),
            _const_spec((1, SG_CH)),
            _const_spec(sg_w.shape),
            _const_spec(sg_bt.shape),
            _const_spec(wout_bf.shape),
        ],
        out_specs=pl.BlockSpec((tm, d), row),
        out_shape=jax.ShapeDtypeStruct((t, d), F32),
        scratch_shapes=[pltpu.VMEM((SCONV_HALO + tm, SCONV_CH), F32)],
        compiler_params=_params(("arbitrary",)),
        name="l1_mixer",
    )(h2, g, win_bf, conv_k, ln_g, ln_b, sg_w, sg_bt, wout_bf)


def _rope_tables(seq):
    half = HEAD_DIM // 2
    inv = ROPE_THETA ** (-jnp.arange(half, dtype=F32) / half)
    ang = jnp.arange(seq, dtype=F32)[:, None] * inv[None, :]
    cos, sin = jnp.cos(ang), jnp.sin(ang)
    reps = LANES // HEAD_DIM
    cos_t = jnp.tile(jnp.concatenate([cos, cos], axis=-1), (1, reps))
    sin_t = jnp.tile(jnp.concatenate([-sin, sin], axis=-1), (1, reps))
    return cos_t, sin_t


def kernel(x, norm_mix_g, norm_ffn_g, even_w_in, even_conv_k, even_conv_b, even_ln_g, even_ln_b,
           even_w_out, odd_w_in, odd_conv_k, odd_ln_g, odd_ln_b, odd_sg_w, odd_sg_b, odd_w_out,
           ffn_w1, ffn_w2, final_g):
    b, s, d = x.shape
    t = b * s
    x2 = x.reshape(t, d)
    cos_t, sin_t = _rope_tables(s)

    a, q, k, v = _in0_call(x2, norm_mix_g[0:1], even_w_in[0].astype(BF16), cos_t, sin_t,
                           even_conv_k[0], even_conv_b[0:1], even_ln_g[0:1], even_ln_b[0:1], s, 256)
    att = _attn_call(q.reshape(b, s, ATTN_WIDTH), k.reshape(b, s, ATTN_WIDTH),
                     v.reshape(b, s, ATTN_WIDTH))
    h = _out0_call(x2, a, att.reshape(t, GROUP_WIDTH), even_w_out[0].astype(BF16), 512)
    h = _ffn_call(h, norm_ffn_g[0:1], ffn_w1[0].astype(BF16), ffn_w2[0].astype(BF16),
                  final_g[None, :], False, 512)
    h = _l1_call(h, norm_mix_g[1:2], odd_w_in[0].astype(BF16), odd_conv_k[0], odd_ln_g[0:1],
                 odd_ln_b[0:1], odd_sg_w[0], odd_sg_b[0].T, odd_w_out[0].astype(BF16), s, 512)
    h = _ffn_call(h, norm_ffn_g[1:2], ffn_w1[1].astype(BF16), ffn_w2[1].astype(BF16),
                  final_g[None, :], True, 512)
    return h.reshape(b, s, d)
```

```python
import functools
import math

import jax
import jax.numpy as jnp
from jax import lax
from jax.experimental import pallas as pl
from jax.experimental.pallas import tpu as pltpu

F32 = jnp.float32
BF16 = jnp.bfloat16

EPS = 1e-6
LANES = 128
CONV_CH = 512
CONV_WIDTH = 31
HEAD_DIM = 64
HEADS_PER_GROUP = 8
DILATED_PAIRS = ((128, 1), (512, 4), (2048, 16))
N_GROUPS = len(DILATED_PAIRS)
GROUP_WIDTH = HEADS_PER_GROUP * HEAD_DIM
ATTN_WIDTH = N_GROUPS * GROUP_WIDTH
ROPE_THETA = 10000.0
SCONV_CH = 512
SCONV_WIDTH = 3
SG_GROUPS = 4
SG_HEAD = 128
SG_CH = SG_GROUPS * SG_HEAD
CHUNK = 128
BAND = 128
NEG_BIG = -1e30

VMEM_LIMIT = 56 * 1024 * 1024


def _rms(x, g):
    return x * lax.rsqrt(jnp.mean(x * x, axis=-1, keepdims=True) + EPS) * g


def _layer_norm(x, g, b):
    mu = jnp.mean(x, axis=-1, keepdims=True)
    xc = x - mu
    return xc * lax.rsqrt(jnp.mean(xc * xc, axis=-1, keepdims=True) + EPS) * g + b


def _const_spec(shape):
    return pl.BlockSpec(shape, lambda *_: (0,) * len(shape), pipeline_mode=pl.Buffered(1))


def _params(sem):
    return pltpu.CompilerParams(dimension_semantics=sem, vmem_limit_bytes=VMEM_LIMIT)


SUBLANES = 8
CONV_HALO = 32
CONV_ROWS = 64
CONV_OFF = CONV_HALO - (CONV_WIDTH - 1)


def _in0_kernel(x_ref, g_ref, w_ref, cos_ref, sin_ref, ck_ref, cb_ref, lg_ref, lb_ref,
                a_ref, q_ref, k_ref, v_ref, gbuf, phase, *, blocks_per_seq):
    tm = x_ref.shape[0]
    i = pl.program_id(0)
    hn = _rms(x_ref[...], g_ref[...]).astype(BF16)

    def proj(c0, n):
        return jnp.dot(hn, w_ref[:, c0:c0 + n], preferred_element_type=F32)

    @pl.when(i % blocks_per_seq == 0)
    def _():
        gbuf[0:CONV_HALO, :] = jnp.zeros((CONV_HALO, CONV_CH), F32)

    gbuf[CONV_HALO:CONV_HALO + tm, :] = proj(0, CONV_CH) * jax.nn.sigmoid(proj(CONV_CH, CONV_CH))
    span = tm + CONV_HALO - SUBLANES
    for b in range(1, SUBLANES):
        phase[b - 1, 0:span, :] = gbuf[b:b + span, :]
    bias = cb_ref[...]
    lg = lg_ref[...]
    lb = lb_ref[...]

    def conv_block(r0):
        acc = jnp.zeros((CONV_ROWS, CONV_CH), F32)
        for j in range(CONV_WIDTH):
            a8, b = divmod(j + CONV_OFF, SUBLANES)
            lo = r0 + a8 * SUBLANES
            if b == 0:
                tap = gbuf[lo:lo + CONV_ROWS, :]
            else:
                tap = phase[b - 1, lo:lo + CONV_ROWS, :]
            acc = acc + tap * ck_ref[j:j + 1, :]
        y = _layer_norm(acc + bias, lg, lb)
        a_ref[r0:r0 + CONV_ROWS, :] = (y * jax.nn.sigmoid(y)).astype(a_ref.dtype)

    cos = cos_ref[...]
    sin = sin_ref[...]
    lane = lax.broadcasted_iota(jnp.int32, (tm, LANES), 1)
    first_half = (lane % HEAD_DIM) < (HEAD_DIM // 2)

    def rope(z):
        rot = jnp.where(first_half,
                        pltpu.roll(z, LANES - HEAD_DIM // 2, 1),
                        pltpu.roll(z, HEAD_DIM // 2, 1))
        return z * cos + rot * sin

    base = 2 * CONV_CH

    def proj_group(c):
        zq = proj(base + c * GROUP_WIDTH, GROUP_WIDTH)
        zk = proj(base + ATTN_WIDTH + c * GROUP_WIDTH, GROUP_WIDTH)
        for s in range(GROUP_WIDTH // LANES):
            lo = c * GROUP_WIDTH + s * LANES
            q_ref[:, lo:lo + LANES] = rope(zq[:, s * LANES:(s + 1) * LANES]) * (HEAD_DIM ** -0.5)
            k_ref[:, lo:lo + LANES] = rope(zk[:, s * LANES:(s + 1) * LANES])
        v_ref[:, c * GROUP_WIDTH:(c + 1) * GROUP_WIDTH] = proj(
            base + 2 * ATTN_WIDTH + c * GROUP_WIDTH, GROUP_WIDTH)

    conv_starts = list(range(0, tm, CONV_ROWS))
    per_group = -(-len(conv_starts) // N_GROUPS)
    for c in range(N_GROUPS):
        proj_group(c)
        for r0 in conv_starts[c * per_group:(c + 1) * per_group]:
            conv_block(r0)
    gbuf[0:CONV_HALO, :] = gbuf[tm:tm + CONV_HALO, :]


def _in0_call(x2, g, w_bf, cos_t, sin_t, conv_k, conv_b, ln_g, ln_b, seq, tm):
    t, d = x2.shape
    n_in = w_bf.shape[1]
    blocks_per_seq = seq // tm
    row = lambda i: (i, 0)
    return pl.pallas_call(
        functools.partial(_in0_kernel, blocks_per_seq=blocks_per_seq),
        grid=(t // tm,),
        in_specs=[
            pl.BlockSpec((tm, d), row),
            _const_spec((1, d)),
            _const_spec((d, n_in)),
            pl.BlockSpec((tm, LANES), lambda i: (i % blocks_per_seq, 0)),
            pl.BlockSpec((tm, LANES), lambda i: (i % blocks_per_seq, 0)),
            _const_spec((CONV_WIDTH, CONV_CH)),
            _const_spec((1, CONV_CH)),
            _const_spec((1, CONV_CH)),
            _const_spec((1, CONV_CH)),
        ],
        out_specs=[
            pl.BlockSpec((tm, CONV_CH), row),
            pl.BlockSpec((tm, ATTN_WIDTH), row),
            pl.BlockSpec((tm, ATTN_WIDTH), row),
            pl.BlockSpec((tm, ATTN_WIDTH), row),
        ],
        out_shape=[
            jax.ShapeDtypeStruct((t, CONV_CH), BF16),
            jax.ShapeDtypeStruct((t, ATTN_WIDTH), F32),
            jax.ShapeDtypeStruct((t, ATTN_WIDTH), F32),
            jax.ShapeDtypeStruct((t, ATTN_WIDTH), F32),
        ],
        scratch_shapes=[
            pltpu.VMEM((CONV_HALO + tm, CONV_CH), F32),
            pltpu.VMEM((SUBLANES - 1, CONV_HALO + tm - SUBLANES, CONV_CH), F32),
        ],
        compiler_params=_params(("arbitrary",)),
        name="l0_in_proj",
    )(x2, g, w_bf, cos_t, sin_t, conv_k, conv_b, ln_g, ln_b)


ATTN_UNROLL = 4


def _attn_kernel(q0, k0, v0, q1, k1, v1, q2, k2, v2, out_ref, acc_o, acc_m, acc_l):
    seq = out_ref.shape[0]
    lane = lax.broadcasted_iota(jnp.int32, (BAND, LANES), 1)
    head0 = lane < HEAD_DIM
    row2 = lax.broadcasted_iota(jnp.int32, (2 * BAND, 2 * BAND), 0) & (BAND - 1)
    col2 = lax.broadcasted_iota(jnp.int32, (2 * BAND, 2 * BAND), 1)
    allowed = jnp.logical_or(jnp.logical_and(col2 < BAND, col2 >= row2),
                             jnp.logical_and(col2 >= BAND, (col2 - BAND) <= row2))
    bias_pc = jnp.where(allowed, 0.0, NEG_BIG).astype(F32)
    bias_noprev = jnp.where(col2 < BAND, NEG_BIG, 0.0).astype(F32)
    row1 = lax.broadcasted_iota(jnp.int32, (2 * BAND, BAND), 0) & (BAND - 1)
    col1 = lax.broadcasted_iota(jnp.int32, (2 * BAND, BAND), 1)
    bias_c = jnp.where(col1 <= row1, 0.0, NEG_BIG).astype(F32)

    def attend(q, kb, vb, bias):
        qs = jnp.concatenate([jnp.where(head0, q, 0.0), jnp.where(head0, 0.0, q)],
                             axis=0).astype(BF16)
        s = lax.dot_general(qs, kb, (((1,), (1,)), ((), ())), preferred_element_type=F32) + bias
        m = jnp.max(s, axis=-1, keepdims=True)
        p = jnp.exp(s - m)
        l = jnp.sum(p, axis=-1, keepdims=True)
        o = jnp.dot(p.astype(BF16), vb, preferred_element_type=F32)
        o = jnp.where(head0, o[:BAND], o[BAND:])
        m = jnp.where(head0, m[:BAND], m[BAND:])
        l = jnp.where(head0, l[:BAND], l[BAND:])
        return o, m, l

    groups = ((q0, k0, v0), (q1, k1, v1), (q2, k2, v2))
    order = sorted(range(N_GROUPS), key=lambda g: -DILATED_PAIRS[g][1])
    for pos, g in enumerate(order):
        dil = DILATED_PAIRS[g][1]
        q_ref, k_ref, v_ref = groups[g]
        nb = (seq // dil) // BAND
        assert nb == 1 or nb % ATTN_UNROLL == 0
        span = dil * BAND
        first, last = pos == 0, pos == N_GROUPS - 1
        runs = max(nb // ATTN_UNROLL, 1)

        def rows(start, dil=dil):
            return pl.ds(start, BAND) if dil == 1 else pl.ds(start, BAND, stride=dil)

        def emit(idx, o, m, l, first=first, last=last):
            if first:
                acc_o[idx, :] = o
                acc_m[idx, :] = m
                acc_l[idx, :] = l
                return
            m_old = acc_m[idx, :]
            m_new = jnp.maximum(m_old, m)
            a_old = jnp.exp(m_old - m_new)
            a_new = jnp.exp(m - m_new)
            o_new = acc_o[idx, :] * a_old + o * a_new
            l_new = acc_l[idx, :] * a_old + l * a_new
            if last:
                out_ref[idx, :] = o_new / l_new
            else:
                acc_o[idx, :] = o_new
                acc_l[idx, :] = l_new
                acc_m[idx, :] = m_new

        def body(it, carry, nb=nb, span=span, runs=runs, rows=rows, emit=emit,
                 q_ref=q_ref, k_ref=k_ref, v_ref=v_ref):
            if nb == 1:
                for u in range(ATTN_UNROLL):
                    idx = rows(it * ATTN_UNROLL + u)
                    o, m, l = attend(q_ref[idx, :], k_ref[idx, :].astype(BF16),
                                     v_ref[idx, :].astype(BF16), bias_c)
                    emit(idx, o, m, l)
                return carry
            r = it // runs
            n0 = (it % runs) * ATTN_UNROLL
            base = r + n0 * span
            has_prev = runs > 1
            kbs, vbs = [], []
            for j in range(-1 if has_prev else 0, ATTN_UNROLL):
                idx = rows(jnp.maximum(base + j * span, 0) if j < 0 else base + j * span)
                kbs.append(k_ref[idx, :].astype(BF16))
                vbs.append(v_ref[idx, :].astype(BF16))
            off = 1 if has_prev else 0
            for u in range(ATTN_UNROLL):
                idx = rows(base + u * span)
                if u == 0 and not has_prev:
                    o, m, l = attend(q_ref[idx, :], kbs[0], vbs[0], bias_c)
                else:
                    bias = bias_pc
                    if u == 0:
                        bias = bias + jnp.where(n0 > 0, 0.0, 1.0) * bias_noprev
                    o, m, l = attend(q_ref[idx, :],
                                     jnp.concatenate([kbs[u + off - 1], kbs[u + off]], axis=0),
                                     jnp.concatenate([vbs[u + off - 1], vbs[u + off]], axis=0), bias)
                emit(idx, o, m, l)
            return carry

        n_iter = dil // ATTN_UNROLL if nb == 1 else dil * runs
        lax.fori_loop(0, n_iter, body, 0)


def _attn_call(q3, k3, v3):
    b, s, _ = q3.shape
    pairs = GROUP_WIDTH // LANES
    in_specs = []
    args = []
    for g in range(N_GROUPS):
        for arr in (q3, k3, v3):
            in_specs.append(pl.BlockSpec((None, s, LANES),
                                         lambda bi, p, g=g: (bi, 0, g * pairs + p)))
            args.append(arr)
    return pl.pallas_call(
        _attn_kernel,
        grid=(b, pairs),
        in_specs=in_specs,
        out_specs=pl.BlockSpec((None, s, LANES), lambda bi, p: (bi, 0, p)),
        out_shape=jax.ShapeDtypeStruct((b, s, GROUP_WIDTH), F32),
        scratch_shapes=[pltpu.VMEM((s, LANES), F32)] * 3,
        compiler_params=_params(("parallel", "parallel")),
        name="l0_dilated_attention",
    )(*args)


FFN_CHUNK = 1024


def _ffn_tail(h, g_ref, w1_ref, w2_ref):
    hn = _rms(h, g_ref[...]).astype(BF16)
    d_ff = w1_ref.shape[1]
    acc = h
    for c0 in range(0, d_ff, FFN_CHUNK):
        a = jnp.dot(hn, w1_ref[:, c0:c0 + FFN_CHUNK], preferred_element_type=F32)
        a = jnp.square(jnp.maximum(a, 0.0)).astype(BF16)
        acc = acc + jnp.dot(a, w2_ref[c0:c0 + FFN_CHUNK, :], preferred_element_type=F32)
    return acc


def _out0_kernel(x_ref, a_ref, att_ref, wout_ref, g_ref, w1_ref, w2_ref, o_ref):
    c = a_ref.shape[1]
    mix = jnp.dot(a_ref[...].astype(BF16), wout_ref[0:c, :], preferred_element_type=F32)
    mix = mix + jnp.dot(att_ref[...].astype(BF16), wout_ref[c:, :], preferred_element_type=F32)
    o_ref[...] = _ffn_tail(x_ref[...] + mix, g_ref, w1_ref, w2_ref)


def _out0_call(x2, a2, att2, wout_bf, g, w1_bf, w2_bf, tm):
    t, d = x2.shape
    row = lambda i: (i, 0)
    return pl.pallas_call(
        _out0_kernel,
        grid=(t // tm,),
        in_specs=[
            pl.BlockSpec((tm, d), row),
            pl.BlockSpec((tm, a2.shape[1]), row),
            pl.BlockSpec((tm, att2.shape[1]), row),
            _const_spec(wout_bf.shape),
            _const_spec((1, d)),
            _const_spec(w1_bf.shape),
            _const_spec(w2_bf.shape),
        ],
        out_specs=pl.BlockSpec((tm, d), row),
        out_shape=jax.ShapeDtypeStruct((t, d), F32),
        compiler_params=_params(("parallel",)),
        name="l0_out_proj_ffn",
    )(x2, a2, att2, wout_bf, g, w1_bf, w2_bf)


SCONV_HALO = 8
GELU_C = math.sqrt(2.0 / math.pi)


def _gelu_tanh(x):
    return 0.5 * x * (1.0 + jnp.tanh(GELU_C * (x + 0.044715 * (x * x * x))))


def _l1_kernel(h_ref, g_ref, win_ref, ck_ref, lg_ref, lb_ref, sgw_ref, sgbt_ref, wout_ref,
               fg_ref, w1_ref, w2_ref, final_g_ref, o_ref, ybuf, *, blocks_per_seq):
    tm = h_ref.shape[0]
    i = pl.program_id(0)
    h = h_ref[...]
    hn = _rms(h, g_ref[...]).astype(BF16)

    def proj(c0, n):
        return jnp.dot(hn, win_ref[:, c0:c0 + n], preferred_element_type=F32)

    gb = proj(0, SCONV_CH)
    y = proj(SCONV_CH, SCONV_CH) * proj(2 * SCONV_CH, SCONV_CH)

    @pl.when(i % blocks_per_seq == 0)
    def _():
        ybuf[0:SCONV_HALO, :] = jnp.zeros((SCONV_HALO, SCONV_CH), F32)

    ybuf[SCONV_HALO:SCONV_HALO + tm, :] = y
    conv = y * ck_ref[SCONV_WIDTH - 1:SCONV_WIDTH, :]
    for j in range(SCONV_WIDTH - 1):
        sh = SCONV_WIDTH - 1 - j
        conv = conv + ybuf[SCONV_HALO - sh:SCONV_HALO - sh + tm, :] * ck_ref[j:j + 1, :]
    ybuf[0:SCONV_HALO, :] = y[tm - SCONV_HALO:tm, :]
    c_out = (gb * conv).astype(BF16)

    u = _gelu_tanh(proj(3 * SCONV_CH, SG_CH))
    v = _gelu_tanh(proj(3 * SCONV_CH + SG_CH, SG_CH))
    v = _layer_norm(v, lg_ref[...], lb_ref[...]).astype(BF16)
    rr = lax.broadcasted_iota(jnp.int32, (CHUNK, CHUNK), 0)
    cc = lax.broadcasted_iota(jnp.int32, (CHUNK, CHUNK), 1)
    causal = rr >= cc
    d_cols = []
    for gi in range(SG_GROUPS):
        ws = jnp.where(causal, sgw_ref[gi], 0.0).astype(BF16)
        bcol = sgbt_ref[:, gi:gi + 1]
        chunks = []
        for n in range(tm // CHUNK):
            vv = v[n * CHUNK:(n + 1) * CHUNK, gi * SG_HEAD:(gi + 1) * SG_HEAD]
            chunks.append(jnp.dot(ws, vv, preferred_element_type=F32) + bcol)
        d_cols.append(jnp.concatenate(chunks, axis=0))
    d_out = (u * jnp.concatenate(d_cols, axis=1)).astype(BF16)

    mix = jnp.dot(c_out, wout_ref[0:SCONV_CH, :], preferred_element_type=F32)
    mix = mix + jnp.dot(d_out, wout_ref[SCONV_CH:, :], preferred_element_type=F32)
    o_ref[...] = _rms(_ffn_tail(h + mix, fg_ref, w1_ref, w2_ref), final_g_ref[...])


def _l1_call(h2, g, win_bf, conv_k, ln_g, ln_b, sg_w, sg_bt, wout_bf, ffn_g, w1_bf, w2_bf,
             final_g, seq, tm):
    t, d = h2.shape
    row = lambda i: (i, 0)
    return pl.pallas_call(
        functools.partial(_l1_kernel, blocks_per_seq=seq // tm),
        grid=(t // tm,),
        in_specs=[
            pl.BlockSpec((tm, d), row),
            _const_spec((1, d)),
            _const_spec(win_bf.shape),
            _const_spec(conv_k.shape),
            _const_spec((1, SG_CH)),
            _const_spec((1, SG_CH)),
            _const_spec(sg_w.shape),
            _const_spec(sg_bt.shape),
            _const_spec(wout_bf.shape),
            _const_spec((1, d)),
            _const_spec(w1_bf.shape),
            _const_spec(w2_bf.shape),
            _const_spec((1, d)),
        ],
        out_specs=pl.BlockSpec((tm, d), row),
        out_shape=jax.ShapeDtypeStruct((t, d), F32),
        scratch_shapes=[pltpu.VMEM((SCONV_HALO + tm, SCONV_CH), F32)],
        compiler_params=_params(("arbitrary",)),
        name="l1_mixer_ffn_final",
    )(h2, g, win_bf, conv_k, ln_g, ln_b, sg_w, sg_bt, wout_bf, ffn_g, w1_bf, w2_bf, final_g)


def _rope_tables(seq):
    half = HEAD_DIM // 2
    inv = ROPE_THETA ** (-jnp.arange(half, dtype=F32) / half)
    ang = jnp.arange(seq, dtype=F32)[:, None] * inv[None, :]
    cos, sin = jnp.cos(ang), jnp.sin(ang)
    reps = LANES // HEAD_DIM
    cos_t = jnp.tile(jnp.concatenate([cos, cos], axis=-1), (1, reps))
    sin_t = jnp.tile(jnp.concatenate([-sin, sin], axis=-1), (1, reps))
    return cos_t, sin_t


def kernel(x, norm_mix_g, norm_ffn_g, even_w_in, even_conv_k, even_conv_b, even_ln_g, even_ln_b,
           even_w_out, odd_w_in, odd_conv_k, odd_ln_g, odd_ln_b, odd_sg_w, odd_sg_b, odd_w_out,
           ffn_w1, ffn_w2, final_g):
    b, s, d = x.shape
    t = b * s
    x2 = x.reshape(t, d)
    cos_t, sin_t = _rope_tables(s)

    a, q, k, v = _in0_call(x2, norm_mix_g[0:1], even_w_in[0].astype(BF16), cos_t, sin_t,
                           even_conv_k[0], even_conv_b[0:1], even_ln_g[0:1], even_ln_b[0:1], s, 256)
    att = _attn_call(q.reshape(b, s, ATTN_WIDTH), k.reshape(b, s, ATTN_WIDTH),
                     v.reshape(b, s, ATTN_WIDTH))
    h = _out0_call(x2, a, att.reshape(t, GROUP_WIDTH), even_w_out[0].astype(BF16),
                   norm_ffn_g[0:1], ffn_w1[0].astype(BF16), ffn_w2[0].astype(BF16), 512)
    h = _l1_call(h, norm_mix_g[1:2], odd_w_in[0].astype(BF16), odd_conv_k[0], odd_ln_g[0:1],
                 odd_ln_b[0:1], odd_sg_w[0], odd_sg_b[0].T, odd_w_out[0].astype(BF16),
                 norm_ffn_g[1:2], ffn_w1[1].astype(BF16), ffn_w2[1].astype(BF16),
                 final_g[None, :], s, 512)
    return h.reshape(b, s, d)
```

```python
import functools
import math

import jax
import jax.numpy as jnp
from jax import lax
from jax.experimental import pallas as pl
from jax.experimental.pallas import tpu as pltpu

F32 = jnp.float32
BF16 = jnp.bfloat16

EPS = 1e-6
LANES = 128
CONV_CH = 512
CONV_WIDTH = 31
HEAD_DIM = 64
HEADS_PER_GROUP = 8
DILATED_PAIRS = ((128, 1), (512, 4), (2048, 16))
N_GROUPS = len(DILATED_PAIRS)
GROUP_WIDTH = HEADS_PER_GROUP * HEAD_DIM
ATTN_WIDTH = N_GROUPS * GROUP_WIDTH
ROPE_THETA = 10000.0
SCONV_CH = 512
SCONV_WIDTH = 3
SG_GROUPS = 4
SG_HEAD = 128
SG_CH = SG_GROUPS * SG_HEAD
CHUNK = 128
BAND = 128
NEG_BIG = -1e30

VMEM_LIMIT = 56 * 1024 * 1024


def _rms(x, g):
    return x * lax.rsqrt(jnp.mean(x * x, axis=-1, keepdims=True) + EPS) * g


def _layer_norm(x, g, b):
    mu = jnp.mean(x, axis=-1, keepdims=True)
    xc = x - mu
    return xc * lax.rsqrt(jnp.mean(xc * xc, axis=-1, keepdims=True) + EPS) * g + b


def _const_spec(shape):
    return pl.BlockSpec(shape, lambda *_: (0,) * len(shape), pipeline_mode=pl.Buffered(1))


def _layer_spec(stacked_shape, layer):
    return pl.BlockSpec((None,) + tuple(stacked_shape[1:]), lambda *_: (layer, 0, 0),
                        pipeline_mode=pl.Buffered(1))


def _params(sem):
    return pltpu.CompilerParams(dimension_semantics=sem, vmem_limit_bytes=VMEM_LIMIT)


SUBLANES = 8
CONV_HALO = 32
CONV_ROWS = 64
CONV_OFF = CONV_HALO - (CONV_WIDTH - 1)


def _in0_kernel(x_ref, g_ref, w_ref, cos_ref, sin_ref, ck_ref, cb_ref, lg_ref, lb_ref,
                a_ref, q_ref, k_ref, v_ref, gbuf, phase, *, blocks_per_seq):
    tm = x_ref.shape[0]
    i = pl.program_id(0)
    hn = _rms(x_ref[...], g_ref[...]).astype(BF16)

    def proj(c0, n):
        return jnp.dot(hn, w_ref[:, c0:c0 + n].astype(BF16), preferred_element_type=F32)

    @pl.when(i % blocks_per_seq == 0)
    def _():
        gbuf[0:CONV_HALO, :] = jnp.zeros((CONV_HALO, CONV_CH), F32)

    gbuf[CONV_HALO:CONV_HALO + tm, :] = proj(0, CONV_CH) * jax.nn.sigmoid(proj(CONV_CH, CONV_CH))
    span = tm + CONV_HALO - SUBLANES
    for b in range(1, SUBLANES):
        phase[b - 1, 0:span, :] = gbuf[b:b + span, :]
    bias = cb_ref[...]
    lg = lg_ref[...]
    lb = lb_ref[...]

    def conv_block(r0):
        acc = jnp.zeros((CONV_ROWS, CONV_CH), F32)
        for j in range(CONV_WIDTH):
            a8, b = divmod(j + CONV_OFF, SUBLANES)
            lo = r0 + a8 * SUBLANES
            if b == 0:
                tap = gbuf[lo:lo + CONV_ROWS, :]
            else:
                tap = phase[b - 1, lo:lo + CONV_ROWS, :]
            acc = acc + tap * ck_ref[j:j + 1, :]
        y = _layer_norm(acc + bias, lg, lb)
        a_ref[r0:r0 + CONV_ROWS, :] = (y * jax.nn.sigmoid(y)).astype(a_ref.dtype)

    cos = cos_ref[...]
    sin = sin_ref[...]
    lane = lax.broadcasted_iota(jnp.int32, (tm, LANES), 1)
    first_half = (lane % HEAD_DIM) < (HEAD_DIM // 2)

    def rope(z):
        rot = jnp.where(first_half,
                        pltpu.roll(z, LANES - HEAD_DIM // 2, 1),
                        pltpu.roll(z, HEAD_DIM // 2, 1))
        return z * cos + rot * sin

    base = 2 * CONV_CH

    def proj_group(c):
        zq = proj(base + c * GROUP_WIDTH, GROUP_WIDTH)
        zk = proj(base + ATTN_WIDTH + c * GROUP_WIDTH, GROUP_WIDTH)
        for s in range(GROUP_WIDTH // LANES):
            lo = c * GROUP_WIDTH + s * LANES
            q_ref[:, lo:lo + LANES] = rope(zq[:, s * LANES:(s + 1) * LANES]) * (HEAD_DIM ** -0.5)
            k_ref[:, lo:lo + LANES] = rope(zk[:, s * LANES:(s + 1) * LANES])
        v_ref[:, c * GROUP_WIDTH:(c + 1) * GROUP_WIDTH] = proj(
            base + 2 * ATTN_WIDTH + c * GROUP_WIDTH, GROUP_WIDTH)

    conv_starts = list(range(0, tm, CONV_ROWS))
    per_group = -(-len(conv_starts) // N_GROUPS)
    for c in range(N_GROUPS):
        proj_group(c)
        for r0 in conv_starts[c * per_group:(c + 1) * per_group]:
            conv_block(r0)
    gbuf[0:CONV_HALO, :] = gbuf[tm:tm + CONV_HALO, :]


def _in0_call(x2, g, w_in, layer, cos_t, sin_t, conv_k, conv_b, ln_g, ln_b, seq, tm):
    t, d = x2.shape
    blocks_per_seq = seq // tm
    row = lambda i: (i, 0)
    return pl.pallas_call(
        functools.partial(_in0_kernel, blocks_per_seq=blocks_per_seq),
        grid=(t // tm,),
        in_specs=[
            pl.BlockSpec((tm, d), row),
            _const_spec((1, d)),
            _layer_spec(w_in.shape, layer),
            pl.BlockSpec((tm, LANES), lambda i: (i % blocks_per_seq, 0)),
            pl.BlockSpec((tm, LANES), lambda i: (i % blocks_per_seq, 0)),
            _const_spec((CONV_WIDTH, CONV_CH)),
            _const_spec((1, CONV_CH)),
            _const_spec((1, CONV_CH)),
            _const_spec((1, CONV_CH)),
        ],
        out_specs=[
            pl.BlockSpec((tm, CONV_CH), row),
            pl.BlockSpec((tm, ATTN_WIDTH), row),
            pl.BlockSpec((tm, ATTN_WIDTH), row),
            pl.BlockSpec((tm, ATTN_WIDTH), row),
        ],
        out_shape=[
            jax.ShapeDtypeStruct((t, CONV_CH), BF16),
            jax.ShapeDtypeStruct((t, ATTN_WIDTH), F32),
            jax.ShapeDtypeStruct((t, ATTN_WIDTH), F32),
            jax.ShapeDtypeStruct((t, ATTN_WIDTH), F32),
        ],
        scratch_shapes=[
            pltpu.VMEM((CONV_HALO + tm, CONV_CH), F32),
            pltpu.VMEM((SUBLANES - 1, CONV_HALO + tm - SUBLANES, CONV_CH), F32),
        ],
        compiler_params=_params(("arbitrary",)),
        name="l0_in_proj",
    )(x2, g, w_in, cos_t, sin_t, conv_k, conv_b, ln_g, ln_b)


ATTN_UNROLL = 4


def _attn_kernel(q0, k0, v0, q1, k1, v1, q2, k2, v2, out_ref, acc_o, acc_m, acc_l):
    seq = out_ref.shape[0]
    lane = lax.broadcasted_iota(jnp.int32, (BAND, LANES), 1)
    head0 = lane < HEAD_DIM
    row2 = lax.broadcasted_iota(jnp.int32, (2 * BAND, 2 * BAND), 0) & (BAND - 1)
    col2 = lax.broadcasted_iota(jnp.int32, (2 * BAND, 2 * BAND), 1)
    allowed = jnp.logical_or(jnp.logical_and(col2 < BAND, col2 >= row2),
                             jnp.logical_and(col2 >= BAND, (col2 - BAND) <= row2))
    bias_pc = jnp.where(allowed, 0.0, NEG_BIG).astype(F32)
    bias_noprev = jnp.where(col2 < BAND, NEG_BIG, 0.0).astype(F32)
    row1 = lax.broadcasted_iota(jnp.int32, (2 * BAND, BAND), 0) & (BAND - 1)
    col1 = lax.broadcasted_iota(jnp.int32, (2 * BAND, BAND), 1)
    bias_c = jnp.where(col1 <= row1, 0.0, NEG_BIG).astype(F32)

    def attend(q, kb, vb, bias):
        qs = jnp.concatenate([jnp.where(head0, q, 0.0), jnp.where(head0, 0.0, q)],
                             axis=0).astype(BF16)
        s = lax.dot_general(qs, kb, (((1,), (1,)), ((), ())), preferred_element_type=F32) + bias
        m = jnp.max(s, axis=-1, keepdims=True)
        p = jnp.exp(s - m)
        l = jnp.sum(p, axis=-1, keepdims=True)
        o = jnp.dot(p.astype(BF16), vb, preferred_element_type=F32)
        o = jnp.where(head0, o[:BAND], o[BAND:])
        m = jnp.where(head0, m[:BAND], m[BAND:])
        l = jnp.where(head0, l[:BAND], l[BAND:])
        return o, m, l

    groups = ((q0, k0, v0), (q1, k1, v1), (q2, k2, v2))
    order = sorted(range(N_GROUPS), key=lambda g: -DILATED_PAIRS[g][1])
    for pos, g in enumerate(order):
        dil = DILATED_PAIRS[g][1]
        q_ref, k_ref, v_ref = groups[g]
        nb = (seq // dil) // BAND
        assert nb == 1 or nb % ATTN_UNROLL == 0
        span = dil * BAND
        first, last = pos == 0, pos == N_GROUPS - 1
        runs = max(nb // ATTN_UNROLL, 1)

        def rows(start, dil=dil):
            return pl.ds(start, BAND) if dil == 1 else pl.ds(start, BAND, stride=dil)

        def emit(idx, o, m, l, first=first, last=last):
            if first:
                acc_o[idx, :] = o
                acc_m[idx, :] = m
                acc_l[idx, :] = l
                return
            m_old = acc_m[idx, :]
            m_new = jnp.maximum(m_old, m)
            a_old = jnp.exp(m_old - m_new)
            a_new = jnp.exp(m - m_new)
            o_new = acc_o[idx, :] * a_old + o * a_new
            l_new = acc_l[idx, :] * a_old + l * a_new
            if last:
                out_ref[idx, :] = o_new / l_new
            else:
                acc_o[idx, :] = o_new
                acc_l[idx, :] = l_new
                acc_m[idx, :] = m_new

        def body(it, carry, nb=nb, span=span, runs=runs, rows=rows, emit=emit,
                 q_ref=q_ref, k_ref=k_ref, v_ref=v_ref):
            if nb == 1:
                for u in range(ATTN_UNROLL):
                    idx = rows(it * ATTN_UNROLL + u)
                    o, m, l = attend(q_ref[idx, :], k_ref[idx, :].astype(BF16),
                                     v_ref[idx, :].astype(BF16), bias_c)
                    emit(idx, o, m, l)
                return carry
            r = it // runs
            n0 = (it % runs) * ATTN_UNROLL
            base = r + n0 * span
            has_prev = runs > 1
            kbs, vbs = [], []
            for j in range(-1 if has_prev else 0, ATTN_UNROLL):
                idx = rows(jnp.maximum(base + j * span, 0) if j < 0 else base + j * span)
                kbs.append(k_ref[idx, :].astype(BF16))
                vbs.append(v_ref[idx, :].astype(BF16))
            off = 1 if has_prev else 0
            for u in range(ATTN_UNROLL):
                idx = rows(base + u * span)
                if u == 0 and not has_prev:
                    o, m, l = attend(q_ref[idx, :], kbs[0], vbs[0], bias_c)
                else:
                    bias = bias_pc
                    if u == 0:
                        bias = bias + jnp.where(n0 > 0, 0.0, 1.0) * bias_noprev
                    o, m, l = attend(q_ref[idx, :],
                                     jnp.concatenate([kbs[u + off - 1], kbs[u + off]], axis=0),
                                     jnp.concatenate([vbs[u + off - 1], vbs[u + off]], axis=0), bias)
                emit(idx, o, m, l)
            return carry

        n_iter = dil // ATTN_UNROLL if nb == 1 else dil * runs
        lax.fori_loop(0, n_iter, body, 0)


def _attn_call(q3, k3, v3):
    b, s, _ = q3.shape
    pairs = GROUP_WIDTH // LANES
    in_specs = []
    args = []
    for g in range(N_GROUPS):
        for arr in (q3, k3, v3):
            in_specs.append(pl.BlockSpec((None, s, LANES),
                                         lambda bi, p, g=g: (bi, 0, g * pairs + p)))
            args.append(arr)
    return pl.pallas_call(
        _attn_kernel,
        grid=(b, pairs),
        in_specs=in_specs,
        out_specs=pl.BlockSpec((None, s, LANES), lambda bi, p: (bi, 0, p)),
        out_shape=jax.ShapeDtypeStruct((b, s, GROUP_WIDTH), F32),
        scratch_shapes=[pltpu.VMEM((s, LANES), F32)] * 3,
        compiler_params=_params(("parallel", "parallel")),
        name="l0_dilated_attention",
    )(*args)


FFN_CHUNK = 1024


def _ffn_tail(h, g_ref, w1_ref, w2_ref):
    hn = _rms(h, g_ref[...]).astype(BF16)
    d_ff = w1_ref.shape[1]
    acc = h
    for c0 in range(0, d_ff, FFN_CHUNK):
        a = jnp.dot(hn, w1_ref[:, c0:c0 + FFN_CHUNK].astype(BF16), preferred_element_type=F32)
        a = jnp.square(jnp.maximum(a, 0.0)).astype(BF16)
        acc = acc + jnp.dot(a, w2_ref[c0:c0 + FFN_CHUNK, :].astype(BF16),
                            preferred_element_type=F32)
    return acc


def _out0_kernel(x_ref, a_ref, att_ref, wout_ref, g_ref, w1_ref, w2_ref, o_ref):
    c = a_ref.shape[1]
    mix = jnp.dot(a_ref[...].astype(BF16), wout_ref[0:c, :].astype(BF16),
                  preferred_element_type=F32)
    mix = mix + jnp.dot(att_ref[...].astype(BF16), wout_ref[c:, :].astype(BF16),
                        preferred_element_type=F32)
    o_ref[...] = _ffn_tail(x_ref[...] + mix, g_ref, w1_ref, w2_ref)


def _out0_call(x2, a2, att2, w_out, g, w1, w2, layer, tm):
    t, d = x2.shape
    row = lambda i: (i, 0)
    return pl.pallas_call(
        _out0_kernel,
        grid=(t // tm,),
        in_specs=[
            pl.BlockSpec((tm, d), row),
            pl.BlockSpec((tm, a2.shape[1]), row),
            pl.BlockSpec((tm, att2.shape[1]), row),
            _layer_spec(w_out.shape, layer),
            _const_spec((1, d)),
            _layer_spec(w1.shape, 2 * layer),
            _layer_spec(w2.shape, 2 * layer),
        ],
        out_specs=pl.BlockSpec((tm, d), row),
        out_shape=jax.ShapeDtypeStruct((t, d), F32),
        compiler_params=_params(("parallel",)),
        name="l0_out_proj_ffn",
    )(x2, a2, att2, w_out, g, w1, w2)


SCONV_HALO = 8
GELU_C = math.sqrt(2.0 / math.pi)


def _gelu_tanh(x):
    return 0.5 * x * (1.0 + jnp.tanh(GELU_C * (x + 0.044715 * (x * x * x))))


def _l1_kernel(h_ref, g_ref, win_ref, ck_ref, lg_ref, lb_ref, sgw_ref, sgbt_ref, wout_ref,
               fg_ref, w1_ref, w2_ref, final_g_ref, o_ref, ybuf, *, blocks_per_seq):
    tm = h_ref.shape[0]
    i = pl.program_id(0)
    h = h_ref[...]
    hn = _rms(h, g_ref[...]).astype(BF16)

    def proj(c0, n):
        return jnp.dot(hn, win_ref[:, c0:c0 + n], preferred_element_type=F32)

    gb = proj(0, SCONV_CH)
    y = proj(SCONV_CH, SCONV_CH) * proj(2 * SCONV_CH, SCONV_CH)

    @pl.when(i % blocks_per_seq == 0)
    def _():
        ybuf[0:SCONV_HALO, :] = jnp.zeros((SCONV_HALO, SCONV_CH), F32)

    ybuf[SCONV_HALO:SCONV_HALO + tm, :] = y
    conv = y * ck_ref[SCONV_WIDTH - 1:SCONV_WIDTH, :]
    for j in range(SCONV_WIDTH - 1):
        sh = SCONV_WIDTH - 1 - j
        conv = conv + ybuf[SCONV_HALO - sh:SCONV_HALO - sh + tm, :] * ck_ref[j:j + 1, :]
    ybuf[0:SCONV_HALO, :] = y[tm - SCONV_HALO:tm, :]
    c_out = (gb * conv).astype(BF16)

    u = _gelu_tanh(proj(3 * SCONV_CH, SG_CH))
    v = _gelu_tanh(proj(3 * SCONV_CH + SG_CH, SG_CH))
    v = _layer_norm(v, lg_ref[...], lb_ref[...]).astype(BF16)
    rr = lax.broadcasted_iota(jnp.int32, (CHUNK, CHUNK), 0)
    cc = lax.broadcasted_iota(jnp.int32, (CHUNK, CHUNK), 1)
    causal = rr >= cc
    d_cols = []
    for gi in range(SG_GROUPS):
        ws = jnp.where(causal, sgw_ref[gi], 0.0).astype(BF16)
        bcol = sgbt_ref[:, gi:gi + 1]
        chunks = []
        for n in range(tm // CHUNK):
            vv = v[n * CHUNK:(n + 1) * CHUNK, gi * SG_HEAD:(gi + 1) * SG_HEAD]
            chunks.append(jnp.dot(ws, vv, preferred_element_type=F32) + bcol)
        d_cols.append(jnp.concatenate(chunks, axis=0))
    d_out = (u * jnp.concatenate(d_cols, axis=1)).astype(BF16)

    mix = jnp.dot(c_out, wout_ref[0:SCONV_CH, :], preferred_element_type=F32)
    mix = mix + jnp.dot(d_out, wout_ref[SCONV_CH:, :], preferred_element_type=F32)
    o_ref[...] = _rms(_ffn_tail(h + mix, fg_ref, w1_ref, w2_ref), final_g_ref[...])


def _l1_call(h2, g, win_bf, conv_k, ln_g, ln_b, sg_w, sg_bt, wout_bf, ffn_g, w1, w2, ffn_layer,
             final_g, seq, tm):
    t, d = h2.shape
    row = lambda i: (i, 0)
    return pl.pallas_call(
        functools.partial(_l1_kernel, blocks_per_seq=seq // tm),
        grid=(t // tm,),
        in_specs=[
            pl.BlockSpec((tm, d), row),
            _const_spec((1, d)),
            _const_spec(win_bf.shape),
            _const_spec(conv_k.shape),
            _const_spec((1, SG_CH)),
            _const_spec((1, SG_CH)),
            _const_spec(sg_w.shape),
            _const_spec(sg_bt.shape),
            _const_spec(wout_bf.shape),
            _const_spec((1, d)),
            _layer_spec(w1.shape, ffn_layer),
            _layer_spec(w2.shape, ffn_layer),
            _const_spec((1, d)),
        ],
        out_specs=pl.BlockSpec((tm, d), row),
        out_shape=jax.ShapeDtypeStruct((t, d), F32),
        scratch_shapes=[pltpu.VMEM((SCONV_HALO + tm, SCONV_CH), F32)],
        compiler_params=_params(("arbitrary",)),
        name="l1_mixer_ffn_final",
    )(h2, g, win_bf, conv_k, ln_g, ln_b, sg_w, sg_bt, wout_bf, ffn_g, w1, w2, final_g)


def _rope_tables(seq):
    half = HEAD_DIM // 2
    inv = ROPE_THETA ** (-jnp.arange(half, dtype=F32) / half)
    ang = jnp.arange(seq, dtype=F32)[:, None] * inv[None, :]
    cos, sin = jnp.cos(ang), jnp.sin(ang)
    reps = LANES // HEAD_DIM
    cos_t = jnp.tile(jnp.concatenate([cos, cos], axis=-1), (1, reps))
    sin_t = jnp.tile(jnp.concatenate([-sin, sin], axis=-1), (1, reps))
    return cos_t, sin_t


def kernel(x, norm_mix_g, norm_ffn_g, even_w_in, even_conv_k, even_conv_b, even_ln_g, even_ln_b,
           even_w_out, odd_w_in, odd_conv_k, odd_ln_g, odd_ln_b, odd_sg_w, odd_sg_b, odd_w_out,
           ffn_w1, ffn_w2, final_g):
    b, s, d = x.shape
    t = b * s
    x2 = x.reshape(t, d)
    cos_t, sin_t = _rope_tables(s)

    a, q, k, v = _in0_call(x2, norm_mix_g[0:1], even_w_in, 0, cos_t, sin_t,
                           even_conv_k[0], even_conv_b[0:1], even_ln_g[0:1], even_ln_b[0:1], s, 256)
    att = _attn_call(q.reshape(b, s, ATTN_WIDTH), k.reshape(b, s, ATTN_WIDTH),
                     v.reshape(b, s, ATTN_WIDTH))
    h = _out0_call(x2, a, att.reshape(t, GROUP_WIDTH), even_w_out, norm_ffn_g[0:1],
                   ffn_w1, ffn_w2, 0, 512)
    h = _l1_call(h, norm_mix_g[1:2], odd_w_in[0].astype(BF16), odd_conv_k[0], odd_ln_g[0:1],
                 odd_ln_b[0:1], odd_sg_w[0], odd_sg_b[0].T, odd_w_out[0].astype(BF16),
                 norm_ffn_g[1:2], ffn_w1, ffn_w2, 1, final_g[None, :], s, 512)
    return h.reshape(b, s, d)
```

```python
import functools
import math

import jax
import jax.numpy as jnp
from jax import lax
from jax.experimental import pallas as pl
from jax.experimental.pallas import tpu as pltpu

F32 = jnp.float32
BF16 = jnp.bfloat16

EPS = 1e-6
LANES = 128
CONV_CH = 512
CONV_WIDTH = 31
HEAD_DIM = 64
HEADS_PER_GROUP = 8
DILATED_PAIRS = ((128, 1), (512, 4), (2048, 16))
N_GROUPS = len(DILATED_PAIRS)
GROUP_WIDTH = HEADS_PER_GROUP * HEAD_DIM
ATTN_WIDTH = N_GROUPS * GROUP_WIDTH
ROPE_THETA = 10000.0
SCONV_CH = 512
SCONV_WIDTH = 3
SG_GROUPS = 4
SG_HEAD = 128
SG_CH = SG_GROUPS * SG_HEAD
CHUNK = 128
BAND = 128
NEG_BIG = -1e30

VMEM_LIMIT = 56 * 1024 * 1024


def _rms(x, g):
    return x * lax.rsqrt(jnp.mean(x * x, axis=-1, keepdims=True) + EPS) * g


def _layer_norm(x, g, b):
    mu = jnp.mean(x, axis=-1, keepdims=True)
    xc = x - mu
    return xc * lax.rsqrt(jnp.mean(xc * xc, axis=-1, keepdims=True) + EPS) * g + b


def _const_spec(shape):
    return pl.BlockSpec(shape, lambda *_: (0,) * len(shape), pipeline_mode=pl.Buffered(1))


def _layer_spec(stacked_shape, layer):
    return pl.BlockSpec((None,) + tuple(stacked_shape[1:]), lambda *_: (layer, 0, 0),
                        pipeline_mode=pl.Buffered(1))


def _params(sem):
    return pltpu.CompilerParams(dimension_semantics=sem, vmem_limit_bytes=VMEM_LIMIT)


SUBLANES = 8
CONV_HALO = 32
CONV_ROWS = 64
CONV_OFF = CONV_HALO - (CONV_WIDTH - 1)


def _in0_kernel(x_ref, g_ref, w_ref, cos_ref, sin_ref, ck_ref, cb_ref, lg_ref, lb_ref,
                a_ref, q_ref, k_ref, v_ref, gbuf, phase, *, blocks_per_seq):
    tm = x_ref.shape[0]
    i = pl.program_id(0)
    hn = _rms(x_ref[...], g_ref[...]).astype(BF16)

    def proj(c0, n):
        return jnp.dot(hn, w_ref[:, c0:c0 + n].astype(BF16), preferred_element_type=F32)

    @pl.when(i % blocks_per_seq == 0)
    def _():
        gbuf[0:CONV_HALO, :] = jnp.zeros((CONV_HALO, CONV_CH), F32)

    gbuf[CONV_HALO:CONV_HALO + tm, :] = proj(0, CONV_CH) * jax.nn.sigmoid(proj(CONV_CH, CONV_CH))
    span = tm + CONV_HALO - SUBLANES
    for b in range(1, SUBLANES):
        phase[b - 1, 0:span, :] = gbuf[b:b + span, :]
    bias = cb_ref[...]
    lg = lg_ref[...]
    lb = lb_ref[...]

    def conv_block(r0):
        acc = jnp.zeros((CONV_ROWS, CONV_CH), F32)
        for j in range(CONV_WIDTH):
            a8, b = divmod(j + CONV_OFF, SUBLANES)
            lo = r0 + a8 * SUBLANES
            if b == 0:
                tap = gbuf[lo:lo + CONV_ROWS, :]
            else:
                tap = phase[b - 1, lo:lo + CONV_ROWS, :]
            acc = acc + tap * ck_ref[j:j + 1, :]
        y = _layer_norm(acc + bias, lg, lb)
        a_ref[r0:r0 + CONV_ROWS, :] = (y * jax.nn.sigmoid(y)).astype(a_ref.dtype)

    cos = cos_ref[...]
    sin = sin_ref[...]
    lane = lax.broadcasted_iota(jnp.int32, (tm, LANES), 1)
    first_half = (lane % HEAD_DIM) < (HEAD_DIM // 2)

    def rope(z):
        rot = jnp.where(first_half,
                        pltpu.roll(z, LANES - HEAD_DIM // 2, 1),
                        pltpu.roll(z, HEAD_DIM // 2, 1))
        return z * cos + rot * sin

    base = 2 * CONV_CH

    def proj_group(c):
        zq = proj(base + c * GROUP_WIDTH, GROUP_WIDTH)
        zk = proj(base + ATTN_WIDTH + c * GROUP_WIDTH, GROUP_WIDTH)
        for s in range(GROUP_WIDTH // LANES):
            lo = c * GROUP_WIDTH + s * LANES
            q_ref[:, lo:lo + LANES] = rope(zq[:, s * LANES:(s + 1) * LANES]) * (HEAD_DIM ** -0.5)
            k_ref[:, lo:lo + LANES] = rope(zk[:, s * LANES:(s + 1) * LANES])
        v_ref[:, c * GROUP_WIDTH:(c + 1) * GROUP_WIDTH] = proj(
            base + 2 * ATTN_WIDTH + c * GROUP_WIDTH, GROUP_WIDTH)

    conv_starts = list(range(0, tm, CONV_ROWS))
    per_group = -(-len(conv_starts) // N_GROUPS)
    for c in range(N_GROUPS):
        proj_group(c)
        for r0 in conv_starts[c * per_group:(c + 1) * per_group]:
            conv_block(r0)
    gbuf[0:CONV_HALO, :] = gbuf[tm:tm + CONV_HALO, :]


def _in0_call(x2, g, w_in, layer, cos_t, sin_t, conv_k, conv_b, ln_g, ln_b, seq, tm):
    t, d = x2.shape
    blocks_per_seq = seq // tm
    row = lambda i: (i, 0)
    return pl.pallas_call(
        functools.partial(_in0_kernel, blocks_per_seq=blocks_per_seq),
        grid=(t // tm,),
        in_specs=[
            pl.BlockSpec((tm, d), row),
            _const_spec((1, d)),
            _layer_spec(w_in.shape, layer),
            pl.BlockSpec((tm, LANES), lambda i: (i % blocks_per_seq, 0)),
            pl.BlockSpec((tm, LANES), lambda i: (i % blocks_per_seq, 0)),
            _const_spec((CONV_WIDTH, CONV_CH)),
            _const_spec((1, CONV_CH)),
            _const_spec((1, CONV_CH)),
            _const_spec((1, CONV_CH)),
        ],
        out_specs=[
            pl.BlockSpec((tm, CONV_CH), row),
            pl.BlockSpec((tm, ATTN_WIDTH), row),
            pl.BlockSpec((tm, ATTN_WIDTH), row),
            pl.BlockSpec((tm, ATTN_WIDTH), row),
        ],
        out_shape=[
            jax.ShapeDtypeStruct((t, CONV_CH), BF16),
            jax.ShapeDtypeStruct((t, ATTN_WIDTH), F32),
            jax.ShapeDtypeStruct((t, ATTN_WIDTH), F32),
            jax.ShapeDtypeStruct((t, ATTN_WIDTH), F32),
        ],
        scratch_shapes=[
            pltpu.VMEM((CONV_HALO + tm, CONV_CH), F32),
            pltpu.VMEM((SUBLANES - 1, CONV_HALO + tm - SUBLANES, CONV_CH), F32),
        ],
        compiler_params=_params(("arbitrary",)),
        name="l0_in_proj",
    )(x2, g, w_in, cos_t, sin_t, conv_k, conv_b, ln_g, ln_b)


ATTN_UNROLL = 16


def _attn_kernel(q0, k0, v0, q1, k1, v1, q2, k2, v2, out_ref, acc_o, acc_m, acc_l):
    seq = out_ref.shape[0]
    lane = lax.broadcasted_iota(jnp.int32, (BAND, LANES), 1)
    head0 = lane < HEAD_DIM
    row2 = lax.broadcasted_iota(jnp.int32, (2 * BAND, 2 * BAND), 0) & (BAND - 1)
    col2 = lax.broadcasted_iota(jnp.int32, (2 * BAND, 2 * BAND), 1)
    allowed = jnp.logical_or(jnp.logical_and(col2 < BAND, col2 >= row2),
                             jnp.logical_and(col2 >= BAND, (col2 - BAND) <= row2))
    bias_pc = jnp.where(allowed, 0.0, NEG_BIG).astype(F32)
    bias_noprev = jnp.where(col2 < BAND, NEG_BIG, 0.0).astype(F32)
    row1 = lax.broadcasted_iota(jnp.int32, (2 * BAND, BAND), 0) & (BAND - 1)
    col1 = lax.broadcasted_iota(jnp.int32, (2 * BAND, BAND), 1)
    bias_c = jnp.where(col1 <= row1, 0.0, NEG_BIG).astype(F32)

    def attend(q, kb, vb, bias):
        qs = jnp.concatenate([jnp.where(head0, q, 0.0), jnp.where(head0, 0.0, q)],
                             axis=0).astype(BF16)
        s = lax.dot_general(qs, kb, (((1,), (1,)), ((), ())), preferred_element_type=F32) + bias
        m = jnp.max(s, axis=-1, keepdims=True)
        p = jnp.exp(s - m)
        l = jnp.sum(p, axis=-1, keepdims=True)
        o = jnp.dot(p.astype(BF16), vb, preferred_element_type=F32)
        o = jnp.where(head0, o[:BAND], o[BAND:])
        m = jnp.where(head0, m[:BAND], m[BAND:])
        l = jnp.where(head0, l[:BAND], l[BAND:])
        return o, m, l

    groups = ((q0, k0, v0), (q1, k1, v1), (q2, k2, v2))
    order = sorted(range(N_GROUPS), key=lambda g: -DILATED_PAIRS[g][1])
    for pos, g in enumerate(order):
        dil = DILATED_PAIRS[g][1]
        q_ref, k_ref, v_ref = groups[g]
        nb = (seq // dil) // BAND
        span = dil * BAND
        first, last = pos == 0, pos == N_GROUPS - 1
        seg = min(nb, ATTN_UNROLL)
        classes = ATTN_UNROLL // seg
        runs = nb // seg
        assert seg * classes == ATTN_UNROLL and runs * seg == nb and dil % classes == 0

        def rows(start, dil=dil):
            return pl.ds(start, BAND) if dil == 1 else pl.ds(start, BAND, stride=dil)

        def emit(idx, o, m, l, first=first, last=last):
            if first:
                acc_o[idx, :] = o
                acc_m[idx, :] = m
                acc_l[idx, :] = l
                return
            m_old = acc_m[idx, :]
            m_new = jnp.maximum(m_old, m)
            a_old = jnp.exp(m_old - m_new)
            a_new = jnp.exp(m - m_new)
            o_new = acc_o[idx, :] * a_old + o * a_new
            l_new = acc_l[idx, :] * a_old + l * a_new
            if last:
                out_ref[idx, :] = o_new / l_new
            else:
                acc_o[idx, :] = o_new
                acc_l[idx, :] = l_new
                acc_m[idx, :] = m_new

        def body(it, carry, span=span, seg=seg, classes=classes, runs=runs, rows=rows, emit=emit,
                 q_ref=q_ref, k_ref=k_ref, v_ref=v_ref):
            has_prev = runs > 1
            for c in range(classes):
                if runs > 1:
                    r = it // runs
                    n0 = (it % runs) * seg
                else:
                    r = it * classes + c
                    n0 = 0
                base = r + n0 * span
                kbs, vbs = [], []
                for j in range(-1 if has_prev else 0, seg):
                    idx = rows(jnp.maximum(base + j * span, 0) if j < 0 else base + j * span)
                    kbs.append(k_ref[idx, :].astype(BF16))
                    vbs.append(v_ref[idx, :].astype(BF16))
                off = 1 if has_prev else 0
                for u in range(seg):
                    idx = rows(base + u * span)
                    if u == 0 and not has_prev:
                        o, m, l = attend(q_ref[idx, :], kbs[0], vbs[0], bias_c)
                    else:
                        bias = bias_pc
                        if u == 0:
                            bias = bias + jnp.where(n0 > 0, 0.0, 1.0) * bias_noprev
                        o, m, l = attend(
                            q_ref[idx, :],
                            jnp.concatenate([kbs[u + off - 1], kbs[u + off]], axis=0),
                            jnp.concatenate([vbs[u + off - 1], vbs[u + off]], axis=0), bias)
                    emit(idx, o, m, l)
            return carry

        lax.fori_loop(0, dil * nb // ATTN_UNROLL, body, 0)


def _attn_call(q3, k3, v3):
    b, s, _ = q3.shape
    pairs = GROUP_WIDTH // LANES
    in_specs = []
    args = []
    for g in range(N_GROUPS):
        for arr in (q3, k3, v3):
            in_specs.append(pl.BlockSpec((None, s, LANES),
                                         lambda bi, p, g=g: (bi, 0, g * pairs + p)))
            args.append(arr)
    return pl.pallas_call(
        _attn_kernel,
        grid=(b, pairs),
        in_specs=in_specs,
        out_specs=pl.BlockSpec((None, s, LANES), lambda bi, p: (bi, 0, p)),
        out_shape=jax.ShapeDtypeStruct((b, s, GROUP_WIDTH), F32),
        scratch_shapes=[pltpu.VMEM((s, LANES), F32)] * 3,
        compiler_params=_params(("parallel", "parallel")),
        name="l0_dilated_attention",
    )(*args)


FFN_CHUNK = 1024


def _ffn_tail(h, g_ref, w1_ref, w2_ref):
    hn = _rms(h, g_ref[...]).astype(BF16)
    d_ff = w1_ref.shape[1]
    acc = h
    for c0 in range(0, d_ff, FFN_CHUNK):
        a = jnp.dot(hn, w1_ref[:, c0:c0 + FFN_CHUNK].astype(BF16), preferred_element_type=F32)
        a = jnp.square(jnp.maximum(a, 0.0)).astype(BF16)
        acc = acc + jnp.dot(a, w2_ref[c0:c0 + FFN_CHUNK, :].astype(BF16),
                            preferred_element_type=F32)
    return acc


def _out0_kernel(x_ref, a_ref, att_ref, wout_ref, g_ref, w1_ref, w2_ref, o_ref):
    c = a_ref.shape[1]
    mix = jnp.dot(a_ref[...].astype(BF16), wout_ref[0:c, :].astype(BF16),
                  preferred_element_type=F32)
    mix = mix + jnp.dot(att_ref[...].astype(BF16), wout_ref[c:, :].astype(BF16),
                        preferred_element_type=F32)
    o_ref[...] = _ffn_tail(x_ref[...] + mix, g_ref, w1_ref, w2_ref)


def _out0_call(x2, a2, att2, w_out, g, w1, w2, layer, tm):
    t, d = x2.shape
    row = lambda i: (i, 0)
    return pl.pallas_call(
        _out0_kernel,
        grid=(t // tm,),
        in_specs=[
            pl.BlockSpec((tm, d), row),
            pl.BlockSpec((tm, a2.shape[1]), row),
            pl.BlockSpec((tm, att2.shape[1]), row),
            _layer_spec(w_out.shape, layer),
            _const_spec((1, d)),
            _layer_spec(w1.shape, 2 * layer),
            _layer_spec(w2.shape, 2 * layer),
        ],
        out_specs=pl.BlockSpec((tm, d), row),
        out_shape=jax.ShapeDtypeStruct((t, d), F32),
        compiler_params=_params(("parallel",)),
        name="l0_out_proj_ffn",
    )(x2, a2, att2, w_out, g, w1, w2)


SCONV_HALO = 8
GELU_C = math.sqrt(2.0 / math.pi)


def _gelu_tanh(x):
    return 0.5 * x * (1.0 + jnp.tanh(GELU_C * (x + 0.044715 * (x * x * x))))


def _l1_kernel(h_ref, g_ref, win_ref, ck_ref, lg_ref, lb_ref, sgw_ref, sgbt_ref, wout_ref,
               fg_ref, w1_ref, w2_ref, final_g_ref, o_ref, ybuf, *, blocks_per_seq):
    tm = h_ref.shape[0]
    i = pl.program_id(0)
    h = h_ref[...]
    hn = _rms(h, g_ref[...]).astype(BF16)

    def proj(c0, n):
        return jnp.dot(hn, win_ref[:, c0:c0 + n], preferred_element_type=F32)

    gb = proj(0, SCONV_CH)
    y = proj(SCONV_CH, SCONV_CH) * proj(2 * SCONV_CH, SCONV_CH)

    @pl.when(i % blocks_per_seq == 0)
    def _():
        ybuf[0:SCONV_HALO, :] = jnp.zeros((SCONV_HALO, SCONV_CH), F32)

    ybuf[SCONV_HALO:SCONV_HALO + tm, :] = y
    conv = y * ck_ref[SCONV_WIDTH - 1:SCONV_WIDTH, :]
    for j in range(SCONV_WIDTH - 1):
        sh = SCONV_WIDTH - 1 - j
        conv = conv + ybuf[SCONV_HALO - sh:SCONV_HALO - sh + tm, :] * ck_ref[j:j + 1, :]
    ybuf[0:SCONV_HALO, :] = y[tm - SCONV_HALO:tm, :]
    c_out = (gb * conv).astype(BF16)

    u = _gelu_tanh(proj(3 * SCONV_CH, SG_CH))
    v = _gelu_tanh(proj(3 * SCONV_CH + SG_CH, SG_CH))
    v = _layer_norm(v, lg_ref[...], lb_ref[...]).astype(BF16)
    rr = lax.broadcasted_iota(jnp.int32, (CHUNK, CHUNK), 0)
    cc = lax.broadcasted_iota(jnp.int32, (CHUNK, CHUNK), 1)
    causal = rr >= cc
    d_cols = []
    for gi in range(SG_GROUPS):
        ws = jnp.where(causal, sgw_ref[gi], 0.0).astype(BF16)
        bcol = sgbt_ref[:, gi:gi + 1]
        chunks = []
        for n in range(tm // CHUNK):
            vv = v[n * CHUNK:(n + 1) * CHUNK, gi * SG_HEAD:(gi + 1) * SG_HEAD]
            chunks.append(jnp.dot(ws, vv, preferred_element_type=F32) + bcol)
        d_cols.append(jnp.concatenate(chunks, axis=0))
    d_out = (u * jnp.concatenate(d_cols, axis=1)).astype(BF16)

    mix = jnp.dot(c_out, wout_ref[0:SCONV_CH, :], preferred_element_type=F32)
    mix = mix + jnp.dot(d_out, wout_ref[SCONV_CH:, :], preferred_element_type=F32)
    o_ref[...] = _rms(_ffn_tail(h + mix, fg_ref, w1_ref, w2_ref), final_g_ref[...])


def _l1_call(h2, g, win_bf, conv_k, ln_g, ln_b, sg_w, sg_bt, wout_bf, ffn_g, w1, w2, ffn_layer,
             final_g, seq, tm):
    t, d = h2.shape
    row = lambda i: (i, 0)
    return pl.pallas_call(
        functools.partial(_l1_kernel, blocks_per_seq=seq // tm),
        grid=(t // tm,),
        in_specs=[
            pl.BlockSpec((tm, d), row),
            _const_spec((1, d)),
            _const_spec(win_bf.shape),
            _const_spec(conv_k.shape),
            _const_spec((1, SG_CH)),
            _const_spec((1, SG_CH)),
            _const_spec(sg_w.shape),
            _const_spec(sg_bt.shape),
            _const_spec(wout_bf.shape),
            _const_spec((1, d)),
            _layer_spec(w1.shape, ffn_layer),
            _layer_spec(w2.shape, ffn_layer),
            _const_spec((1, d)),
        ],
        out_specs=pl.BlockSpec((tm, d), row),
        out_shape=jax.ShapeDtypeStruct((t, d), F32),
        scratch_shapes=[pltpu.VMEM((SCONV_HALO + tm, SCONV_CH), F32)],
        compiler_params=_params(("arbitrary",)),
        name="l1_mixer_ffn_final",
    )(h2, g, win_bf, conv_k, ln_g, ln_b, sg_w, sg_bt, wout_bf, ffn_g, w1, w2, final_g)


def _rope_tables(seq):
    half = HEAD_DIM // 2
    inv = ROPE_THETA ** (-jnp.arange(half, dtype=F32) / half)
    ang = jnp.arange(seq, dtype=F32)[:, None] * inv[None, :]
    cos, sin = jnp.cos(ang), jnp.sin(ang)
    reps = LANES // HEAD_DIM
    cos_t = jnp.tile(jnp.concatenate([cos, cos], axis=-1), (1, reps))
    sin_t = jnp.tile(jnp.concatenate([-sin, sin], axis=-1), (1, reps))
    return cos_t, sin_t


def kernel(x, norm_mix_g, norm_ffn_g, even_w_in, even_conv_k, even_conv_b, even_ln_g, even_ln_b,
           even_w_out, odd_w_in, odd_conv_k, odd_ln_g, odd_ln_b, odd_sg_w, odd_sg_b, odd_w_out,
           ffn_w1, ffn_w2, final_g):
    b, s, d = x.shape
    t = b * s
    x2 = x.reshape(t, d)
    cos_t, sin_t = _rope_tables(s)

    a, q, k, v = _in0_call(x2, norm_mix_g[0:1], even_w_in, 0, cos_t, sin_t,
                           even_conv_k[0], even_conv_b[0:1], even_ln_g[0:1], even_ln_b[0:1], s, 256)
    att = _attn_call(q.reshape(b, s, ATTN_WIDTH), k.reshape(b, s, ATTN_WIDTH),
                     v.reshape(b, s, ATTN_WIDTH))
    h = _out0_call(x2, a, att.reshape(t, GROUP_WIDTH), even_w_out, norm_ffn_g[0:1],
                   ffn_w1, ffn_w2, 0, 512)
    h = _l1_call(h, norm_mix_g[1:2], odd_w_in[0].astype(BF16), odd_conv_k[0], odd_ln_g[0:1],
                 odd_ln_b[0:1], odd_sg_w[0], odd_sg_b[0].T, odd_w_out[0].astype(BF16),
                 norm_ffn_g[1:2], ffn_w1, ffn_w2, 1, final_g[None, :], s, 512)
    return h.reshape(b, s, d)
```

```python
import functools
import math

import jax
import jax.numpy as jnp
from jax import lax
from jax.experimental import pallas as pl
from jax.experimental.pallas import tpu as pltpu

F32 = jnp.float32
BF16 = jnp.bfloat16

EPS = 1e-6
LANES = 128
CONV_CH = 512
CONV_WIDTH = 31
HEAD_DIM = 64
HEADS_PER_GROUP = 8
DILATED_PAIRS = ((128, 1), (512, 4), (2048, 16))
N_GROUPS = len(DILATED_PAIRS)
GROUP_WIDTH = HEADS_PER_GROUP * HEAD_DIM
ATTN_WIDTH = N_GROUPS * GROUP_WIDTH
ROPE_THETA = 10000.0
SCONV_CH = 512
SCONV_WIDTH = 3
SG_GROUPS = 4
SG_HEAD = 128
SG_CH = SG_GROUPS * SG_HEAD
CHUNK = 128
BAND = 128
NEG_BIG = -1e30
Q_SCALE = HEAD_DIM ** -0.5 * math.log2(math.e)

VMEM_LIMIT = 60 * 1024 * 1024
ROWS_IN_PROJ = 256
ROWS_TAIL = 512


def _rms(x, g):
    return x * lax.rsqrt(jnp.mean(x * x, axis=-1, keepdims=True) + EPS) * g


def _layer_norm(x, g, b):
    mu = jnp.mean(x, axis=-1, keepdims=True)
    xc = x - mu
    return xc * lax.rsqrt(jnp.mean(xc * xc, axis=-1, keepdims=True) + EPS) * g + b


def _const_spec(shape):
    return pl.BlockSpec(shape, lambda *_: (0,) * len(shape), pipeline_mode=pl.Buffered(1))


def _layer_spec(stacked_shape, layer):
    return pl.BlockSpec((None,) + tuple(stacked_shape[1:]), lambda *_: (layer, 0, 0),
                        pipeline_mode=pl.Buffered(1))


def _params(sem):
    return pltpu.CompilerParams(dimension_semantics=sem, vmem_limit_bytes=VMEM_LIMIT)


SUBLANES = 8
CONV_HALO = 32
CONV_ROWS = 64
CONV_OFF = CONV_HALO - (CONV_WIDTH - 1)


def _in0_kernel(x_ref, g_ref, w_ref, cos_ref, sin_ref, ck_ref, cb_ref, lg_ref, lb_ref,
                a_ref, q_ref, k_ref, v_ref, gbuf, phase, *, blocks_per_seq):
    tm = x_ref.shape[0]
    i = pl.program_id(0)
    hn = _rms(x_ref[...], g_ref[...]).astype(BF16)

    def proj(c0, n):
        return jnp.dot(hn, w_ref[:, c0:c0 + n].astype(BF16), preferred_element_type=F32)

    @pl.when(i % blocks_per_seq == 0)
    def _():
        gbuf[0:CONV_HALO, :] = jnp.zeros((CONV_HALO, CONV_CH), F32)

    gbuf[CONV_HALO:CONV_HALO + tm, :] = proj(0, CONV_CH) * jax.nn.sigmoid(proj(CONV_CH, CONV_CH))
    span = tm + CONV_HALO - SUBLANES
    for b in range(1, SUBLANES):
        phase[b - 1, 0:span, :] = gbuf[b:b + span, :]
    bias = cb_ref[...]
    lg = lg_ref[...]
    lb = lb_ref[...]

    def conv_block(r0):
        acc = jnp.zeros((CONV_ROWS // SUBLANES, SUBLANES, CONV_CH), F32)
        for j in range(CONV_WIDTH):
            a8, b = divmod(j + CONV_OFF, SUBLANES)
            lo = r0 + a8 * SUBLANES
            if b == 0:
                tap = gbuf[lo:lo + CONV_ROWS, :]
            else:
                tap = phase[b - 1, lo:lo + CONV_ROWS, :]
            acc = acc + tap.reshape(acc.shape) * ck_ref[j][None]
        y = _layer_norm(acc.reshape(CONV_ROWS, CONV_CH) + bias, lg, lb)
        a_ref[r0:r0 + CONV_ROWS, :] = (y * jax.nn.sigmoid(y)).astype(a_ref.dtype)

    cos = cos_ref[...]
    sin = sin_ref[...]
    lane = lax.broadcasted_iota(jnp.int32, (tm, LANES), 1)
    first_half = (lane % HEAD_DIM) < (HEAD_DIM // 2)

    def rope(z):
        rot = jnp.where(first_half,
                        pltpu.roll(z, LANES - HEAD_DIM // 2, 1),
                        pltpu.roll(z, HEAD_DIM // 2, 1))
        return z * cos + rot * sin

    base = 2 * CONV_CH
    for c in range(N_GROUPS):
        zq = proj(base + c * GROUP_WIDTH, GROUP_WIDTH)
        zk = proj(base + ATTN_WIDTH + c * GROUP_WIDTH, GROUP_WIDTH)
        for s in range(GROUP_WIDTH // LANES):
            lo = c * GROUP_WIDTH + s * LANES
            q_ref[:, lo:lo + LANES] = rope(zq[:, s * LANES:(s + 1) * LANES]) * Q_SCALE
            k_ref[:, lo:lo + LANES] = rope(zk[:, s * LANES:(s + 1) * LANES])
        v_ref[:, c * GROUP_WIDTH:(c + 1) * GROUP_WIDTH] = proj(
            base + 2 * ATTN_WIDTH + c * GROUP_WIDTH, GROUP_WIDTH)

    for r0 in range(0, tm, CONV_ROWS):
        conv_block(r0)
    gbuf[0:CONV_HALO, :] = gbuf[tm:tm + CONV_HALO, :]


def _in0_call(x2, g, w_in, layer, cos_t, sin_t, conv_k, conv_b, ln_g, ln_b, seq, tm):
    t, d = x2.shape
    blocks_per_seq = seq // tm
    row = lambda i: (i, 0)
    return pl.pallas_call(
        functools.partial(_in0_kernel, blocks_per_seq=blocks_per_seq),
        grid=(t // tm,),
        in_specs=[
            pl.BlockSpec((tm, d), row),
            _const_spec((1, d)),
            _layer_spec(w_in.shape, layer),
            pl.BlockSpec((tm, LANES), lambda i: (i % blocks_per_seq, 0)),
            pl.BlockSpec((tm, LANES), lambda i: (i % blocks_per_seq, 0)),
            _const_spec((CONV_WIDTH, SUBLANES, CONV_CH)),
            _const_spec((1, CONV_CH)),
            _const_spec((1, CONV_CH)),
            _const_spec((1, CONV_CH)),
        ],
        out_specs=[
            pl.BlockSpec((tm, CONV_CH), row),
            pl.BlockSpec((tm, ATTN_WIDTH), row),
            pl.BlockSpec((tm, ATTN_WIDTH), row),
            pl.BlockSpec((tm, ATTN_WIDTH), row),
        ],
        out_shape=[
            jax.ShapeDtypeStruct((t, CONV_CH), BF16),
            jax.ShapeDtypeStruct((t, ATTN_WIDTH), F32),
            jax.ShapeDtypeStruct((t, ATTN_WIDTH), F32),
            jax.ShapeDtypeStruct((t, ATTN_WIDTH), F32),
        ],
        scratch_shapes=[
            pltpu.VMEM((CONV_HALO + tm, CONV_CH), F32),
            pltpu.VMEM((SUBLANES - 1, CONV_HALO + tm - SUBLANES, CONV_CH), F32),
        ],
        compiler_params=_params(("arbitrary",)),
        name="l0_in_proj",
    )(x2, g, w_in, cos_t, sin_t, conv_k, conv_b, ln_g, ln_b)


ATTN_CLASSES = 4


def _attn_kernel(q0, k0, v0, q1, k1, v1, q2, k2, v2, out_ref, acc_o, acc_m, acc_l, stg_q, stg_k,
                 stg_v):
    seq = out_ref.shape[0]
    f = ATTN_CLASSES
    sub = seq // f
    piece = BAND // f
    lane = lax.broadcasted_iota(jnp.int32, (BAND, LANES), 1)
    head0 = lane < HEAD_DIM

    def ds(start, size, stride):
        return pl.ds(start, size) if stride == 1 else pl.ds(start, size, stride=stride)

    def load(ref, idxs):
        parts = [ref[ix, :] for ix in idxs]
        return parts[0] if len(parts) == 1 else jnp.concatenate(parts, axis=0)

    def store(ref, idxs, val):
        n = val.shape[0] // len(idxs)
        for i, ix in enumerate(idxs):
            ref[ix, :] = val[i * n:(i + 1) * n]

    def biases(pos):
        row2 = pos(lax.broadcasted_iota(jnp.int32, (2 * BAND, 2 * BAND), 0) & (BAND - 1))
        col2 = lax.broadcasted_iota(jnp.int32, (2 * BAND, 2 * BAND), 1)
        key2 = pos(col2 & (BAND - 1))
        allowed = jnp.logical_or(jnp.logical_and(col2 < BAND, key2 >= row2),
                                 jnp.logical_and(col2 >= BAND, key2 <= row2))
        row1 = pos(lax.broadcasted_iota(jnp.int32, (2 * BAND, BAND), 0) & (BAND - 1))
        key1 = pos(lax.broadcasted_iota(jnp.int32, (2 * BAND, BAND), 1))
        return (jnp.where(allowed, 0.0, NEG_BIG).astype(F32),
                jnp.where(key1 <= row1, 0.0, NEG_BIG).astype(F32))

    bias_pc, bias_c = biases(lambda p: p)
    bias_pc_perm, bias_c_perm = biases(lambda p: f * (p & (piece - 1)) + p // piece)

    def attend(q, kb, vb, bias):
        qs = jnp.concatenate([jnp.where(head0, q, 0.0), jnp.where(head0, 0.0, q)],
                             axis=0).astype(BF16)
        s = lax.dot_general(qs, kb, (((1,), (1,)), ((), ())), preferred_element_type=F32) + bias
        m = jnp.max(s, axis=-1, keepdims=True)
        p = jnp.exp2(s - m)
        l = jnp.sum(p, axis=-1, keepdims=True)
        o = jnp.dot(p.astype(BF16), vb, preferred_element_type=F32)
        o = jnp.where(head0, o[:BAND], o[BAND:])
        m = jnp.where(head0, m[:BAND], m[BAND:])
        l = jnp.where(head0, l[:BAND], l[BAND:])
        return o, m, l

    def emit(first, last, acc_idx, out_idx, o, m, l):
        if first:
            store(acc_o, acc_idx, o)
            store(acc_m, acc_idx, m)
            store(acc_l, acc_idx, l)
            return
        m_old = load(acc_m, acc_idx)
        m_new = jnp.maximum(m_old, m)
        a_old = jnp.exp2(m_old - m_new)
        a_new = jnp.exp2(m - m_new)
        o_new = load(acc_o, acc_idx) * a_old + o * a_new
        l_new = load(acc_l, acc_idx) * a_old + l * a_new
        if last:
            store(out_ref, out_idx, o_new / l_new)
        else:
            store(acc_o, acc_idx, o_new)
            store(acc_l, acc_idx, l_new)
            store(acc_m, acc_idx, m_new)

    groups = ((q0, k0, v0), (q1, k1, v1), (q2, k2, v2))
    order = sorted(range(N_GROUPS), key=lambda g: DILATED_PAIRS[g][1])
    order = order[1:2] + order[2:] + order[:1]
    for pos, g in enumerate(order):
        dil = DILATED_PAIRS[g][1]
        q_ref, k_ref, v_ref = groups[g]
        nb = (seq // dil) // BAND
        first, last = pos == 0, pos == N_GROUPS - 1
        if dil == 1:
            classes = [(lambda n: [ds(BAND * n + c, piece, f) for c in range(f)],
                        lambda n: [ds(c * sub + piece * n, piece, 1) for c in range(f)],
                        lambda n: [ds(BAND * n + c, piece, f) for c in range(f)])]
            srcs = (q_ref, k_ref, v_ref)
            b_pc, b_c = bias_pc_perm, bias_c_perm
        else:
            assert dil % f == 0
            e = dil // f
            if e > 1:
                for ref, stg in ((q_ref, stg_q), (k_ref, stg_k), (v_ref, stg_v)):
                    for c in range(f):
                        stg[c * sub:(c + 1) * sub, :] = ref[pl.ds(c, sub, stride=f), :]
                srcs = (stg_q, stg_k, stg_v)
            else:
                srcs = (q_ref, k_ref, v_ref)
            classes = []
            for r in range(dil):
                acc_rows = (lambda n, r=r: [ds((r % f) * sub + r // f + e * BAND * n, BAND, e)])
                nat_rows = (lambda n, r=r: [ds(r + dil * BAND * n, BAND, dil)])
                classes.append((acc_rows if e > 1 else nat_rows, acc_rows, nat_rows))
            b_pc, b_c = bias_pc, bias_c
        for in_rows, acc_rows, out_rows in classes:
            kbs = [load(srcs[1], in_rows(n)).astype(BF16) for n in range(nb)]
            vbs = [load(srcs[2], in_rows(n)).astype(BF16) for n in range(nb)]
            for n in range(nb):
                q = load(srcs[0], in_rows(n))
                if n == 0:
                    o, m, l = attend(q, kbs[0], vbs[0], b_c)
                else:
                    o, m, l = attend(q, jnp.concatenate([kbs[n - 1], kbs[n]], axis=0),
                                     jnp.concatenate([vbs[n - 1], vbs[n]], axis=0), b_pc)
                emit(first, last, acc_rows(n), out_rows(n), o, m, l)


def _attn_call(q3, k3, v3):
    b, s, _ = q3.shape
    pairs = GROUP_WIDTH // LANES
    in_specs = []
    args = []
    for g in range(N_GROUPS):
        for arr in (q3, k3, v3):
            in_specs.append(pl.BlockSpec((None, s, LANES),
                                         lambda bi, p, g=g: (bi, 0, g * pairs + p)))
            args.append(arr)
    return pl.pallas_call(
        _attn_kernel,
        grid=(b, pairs),
        in_specs=in_specs,
        out_specs=pl.BlockSpec((None, s, LANES), lambda bi, p: (bi, 0, p)),
        out_shape=jax.ShapeDtypeStruct((b, s, GROUP_WIDTH), F32),
        scratch_shapes=[pltpu.VMEM((s, LANES), F32)] * 6,
        compiler_params=_params(("parallel", "parallel")),
        name="l0_dilated_attention",
    )(*args)


FFN_CHUNK = 1024


def _ffn_tail(h, g_ref, w1_ref, w2_ref):
    hn = _rms(h, g_ref[...]).astype(BF16)
    d_ff = w1_ref.shape[1]
    acc = h
    for c0 in range(0, d_ff, FFN_CHUNK):
        a = jnp.dot(hn, w1_ref[:, c0:c0 + FFN_CHUNK].astype(BF16), preferred_element_type=F32)
        a = jnp.square(jnp.maximum(a, 0.0)).astype(BF16)
        acc = acc + jnp.dot(a, w2_ref[c0:c0 + FFN_CHUNK, :].astype(BF16),
                            preferred_element_type=F32)
    return acc


def _out0_kernel(x_ref, a_ref, att_ref, wout_ref, g_ref, w1_ref, w2_ref, o_ref):
    c = a_ref.shape[1]
    mix = jnp.dot(a_ref[...].astype(BF16), wout_ref[0:c, :].astype(BF16),
                  preferred_element_type=F32)
    mix = mix + jnp.dot(att_ref[...].astype(BF16), wout_ref[c:, :].astype(BF16),
                        preferred_element_type=F32)
    o_ref[...] = _ffn_tail(x_ref[...] + mix, g_ref, w1_ref, w2_ref)


def _out0_call(x2, a2, att2, w_out, g, w1, w2, layer, tm):
    t, d = x2.shape
    row = lambda i: (i, 0)
    return pl.pallas_call(
        _out0_kernel,
        grid=(t // tm,),
        in_specs=[
            pl.BlockSpec((tm, d), row),
            pl.BlockSpec((tm, a2.shape[1]), row),
            pl.BlockSpec((tm, att2.shape[1]), row),
            _layer_spec(w_out.shape, layer),
            _const_spec((1, d)),
            _layer_spec(w1.shape, 2 * layer),
            _layer_spec(w2.shape, 2 * layer),
        ],
        out_specs=pl.BlockSpec((tm, d), row),
        out_shape=jax.ShapeDtypeStruct((t, d), F32),
        compiler_params=_params(("parallel",)),
        name="l0_out_proj_ffn",
    )(x2, a2, att2, w_out, g, w1, w2)


SCONV_HALO = 8
GELU_C = math.sqrt(2.0 / math.pi)


def _gelu_tanh(x):
    return 0.5 * x * (1.0 + jnp.tanh(GELU_C * (x + 0.044715 * (x * x * x))))


def _l1_kernel(h_ref, g_ref, win_ref, ck_ref, lg_ref, lb_ref, sgw_ref, sgbt_ref, wout_ref,
               fg_ref, w1_ref, w2_ref, final_g_ref, o_ref, ybuf, *, blocks_per_seq):
    tm = h_ref.shape[0]
    i = pl.program_id(0)
    h = h_ref[...]
    hn = _rms(h, g_ref[...]).astype(BF16)

    def proj(c0, n):
        return jnp.dot(hn, win_ref[:, c0:c0 + n].astype(BF16), preferred_element_type=F32)

    gb = proj(0, SCONV_CH)
    y = proj(SCONV_CH, SCONV_CH) * proj(2 * SCONV_CH, SCONV_CH)

    @pl.when(i % blocks_per_seq == 0)
    def _():
        ybuf[0:SCONV_HALO, :] = jnp.zeros((SCONV_HALO, SCONV_CH), F32)

    ybuf[SCONV_HALO:SCONV_HALO + tm, :] = y
    conv = y * ck_ref[SCONV_WIDTH - 1:SCONV_WIDTH, :]
    for j in range(SCONV_WIDTH - 1):
        sh = SCONV_WIDTH - 1 - j
        conv = conv + ybuf[SCONV_HALO - sh:SCONV_HALO - sh + tm, :] * ck_ref[j:j + 1, :]
    ybuf[0:SCONV_HALO, :] = y[tm - SCONV_HALO:tm, :]
    c_out = (gb * conv).astype(BF16)

    u = _gelu_tanh(proj(3 * SCONV_CH, SG_CH))
    v = _gelu_tanh(proj(3 * SCONV_CH + SG_CH, SG_CH))
    v = _layer_norm(v, lg_ref[...], lb_ref[...]).astype(BF16)
    rr = lax.broadcasted_iota(jnp.int32, (CHUNK, CHUNK), 0)
    cc = lax.broadcasted_iota(jnp.int32, (CHUNK, CHUNK), 1)
    causal = rr >= cc
    d_cols = []
    for gi in range(SG_GROUPS):
        ws = jnp.where(causal, sgw_ref[gi], 0.0).astype(BF16)
        bcol = sgbt_ref[:, gi:gi + 1]
        chunks = []
        for n in range(tm // CHUNK):
            vv = v[n * CHUNK:(n + 1) * CHUNK, gi * SG_HEAD:(gi + 1) * SG_HEAD]
            chunks.append(jnp.dot(ws, vv, preferred_element_type=F32) + bcol)
        d_cols.append(jnp.concatenate(chunks, axis=0))
    d_out = (u * jnp.concatenate(d_cols, axis=1)).astype(BF16)

    mix = jnp.dot(c_out, wout_ref[0:SCONV_CH, :].astype(BF16), preferred_element_type=F32)
    mix = mix + jnp.dot(d_out, wout_ref[SCONV_CH:, :].astype(BF16),
                        preferred_element_type=F32)
    o_ref[...] = _rms(_ffn_tail(h + mix, fg_ref, w1_ref, w2_ref), final_g_ref[...])


def _l1_call(h2, g, w_in, conv_k, ln_g, ln_b, sg_w, sg_bt, w_out, layer, ffn_g, w1, w2, ffn_layer,
             final_g, seq, tm):
    t, d = h2.shape
    row = lambda i: (i, 0)
    return pl.pallas_call(
        functools.partial(_l1_kernel, blocks_per_seq=seq // tm),
        grid=(t // tm,),
        in_specs=[
            pl.BlockSpec((tm, d), row),
            _const_spec((1, d)),
            _layer_spec(w_in.shape, layer),
            _const_spec(conv_k.shape),
            _const_spec((1, SG_CH)),
            _const_spec((1, SG_CH)),
            _const_spec(sg_w.shape),
            _const_spec(sg_bt.shape),
            _layer_spec(w_out.shape, layer),
            _const_spec((1, d)),
            _layer_spec(w1.shape, ffn_layer),
            _layer_spec(w2.shape, ffn_layer),
            _const_spec((1, d)),
        ],
        out_specs=pl.BlockSpec((tm, d), row),
        out_shape=jax.ShapeDtypeStruct((t, d), F32),
        scratch_shapes=[pltpu.VMEM((SCONV_HALO + tm, SCONV_CH), F32)],
        compiler_params=_params(("arbitrary",)),
        name="l1_mixer_ffn_final",
    )(h2, g, w_in, conv_k, ln_g, ln_b, sg_w, sg_bt, w_out, ffn_g, w1, w2, final_g)


def _rope_tables(seq):
    half = HEAD_DIM // 2
    inv = ROPE_THETA ** (-jnp.arange(half, dtype=F32) / half)
    ang = jnp.arange(seq, dtype=F32)[:, None] * inv[None, :]
    cos, sin = jnp.cos(ang), jnp.sin(ang)
    reps = LANES // HEAD_DIM
    cos_t = jnp.tile(jnp.concatenate([cos, cos], axis=-1), (1, reps))
    sin_t = jnp.tile(jnp.concatenate([-sin, sin], axis=-1), (1, reps))
    return cos_t, sin_t


def kernel(x, norm_mix_g, norm_ffn_g, even_w_in, even_conv_k, even_conv_b, even_ln_g, even_ln_b,
           even_w_out, odd_w_in, odd_conv_k, odd_ln_g, odd_ln_b, odd_sg_w, odd_sg_b, odd_w_out,
           ffn_w1, ffn_w2, final_g):
    b, s, d = x.shape
    t = b * s
    x2 = x.reshape(t, d)
    cos_t, sin_t = _rope_tables(s)

    a, q, k, v = _in0_call(x2, norm_mix_g[0:1], even_w_in, 0, cos_t, sin_t,
                           jnp.broadcast_to(even_conv_k[0][:, None, :],
                                            (CONV_WIDTH, SUBLANES, CONV_CH)),
                           even_conv_b[0:1], even_ln_g[0:1], even_ln_b[0:1], s, ROWS_IN_PROJ)
    att = _attn_call(q.reshape(b, s, ATTN_WIDTH), k.reshape(b, s, ATTN_WIDTH),
                     v.reshape(b, s, ATTN_WIDTH))
    h = _out0_call(x2, a, att.reshape(t, GROUP_WIDTH), even_w_out, norm_ffn_g[0:1],
                   ffn_w1, ffn_w2, 0, ROWS_TAIL)
    h = _l1_call(h, norm_mix_g[1:2], odd_w_in, odd_conv_k[0], odd_ln_g[0:1], odd_ln_b[0:1],
                 odd_sg_w[0], odd_sg_b[0].T, odd_w_out.astype(BF16), 0, norm_ffn_g[1:2], ffn_w1, ffn_w2, 1,
                 final_g[None, :], s, ROWS_TAIL)
    return h.reshape(b, s, d)
```

```python
import functools
import math

import jax
import jax.numpy as jnp
from jax import lax
from jax.experimental import pallas as pl
from jax.experimental.pallas import tpu as pltpu

F32 = jnp.float32
BF16 = jnp.bfloat16

EPS = 1e-6
LANES = 128
CONV_CH = 512
CONV_WIDTH = 31
HEAD_DIM = 64
HEADS_PER_GROUP = 8
DILATED_PAIRS = ((128, 1), (512, 4), (2048, 16))
N_GROUPS = len(DILATED_PAIRS)
GROUP_WIDTH = HEADS_PER_GROUP * HEAD_DIM
ATTN_WIDTH = N_GROUPS * GROUP_WIDTH
ROPE_THETA = 10000.0
SCONV_CH = 512
SCONV_WIDTH = 3
SG_GROUPS = 4
SG_HEAD = 128
SG_CH = SG_GROUPS * SG_HEAD
CHUNK = 128
BAND = 128
NEG_BIG = -1e30
Q_SCALE = HEAD_DIM ** -0.5 * math.log2(math.e)

VMEM_LIMIT = 60 * 1024 * 1024
ROWS_IN_PROJ = 512
ROWS_TAIL = 512


def _rms(x, g):
    return x * lax.rsqrt(jnp.mean(x * x, axis=-1, keepdims=True) + EPS) * g


def _layer_norm(x, g, b):
    mu = jnp.mean(x, axis=-1, keepdims=True)
    xc = x - mu
    return xc * lax.rsqrt(jnp.mean(xc * xc, axis=-1, keepdims=True) + EPS) * g + b


def _const_spec(shape):
    return pl.BlockSpec(shape, lambda *_: (0,) * len(shape), pipeline_mode=pl.Buffered(1))


def _layer_spec(stacked_shape, layer):
    return pl.BlockSpec((None,) + tuple(stacked_shape[1:]), lambda *_: (layer, 0, 0),
                        pipeline_mode=pl.Buffered(1))


def _params(sem):
    return pltpu.CompilerParams(dimension_semantics=sem, vmem_limit_bytes=VMEM_LIMIT)


SUBLANES = 8
CONV_HALO = 32
CONV_ROWS = 64
CONV_OFF = CONV_HALO - (CONV_WIDTH - 1)


def _in0_kernel(x_ref, g_ref, w_ref, cos_ref, sin_ref, ck_ref, cb_ref, lg_ref, lb_ref,
                a_ref, q_ref, k_ref, v_ref, gbuf, phase, *, blocks_per_seq):
    tm = x_ref.shape[0]
    i = pl.program_id(0)
    hn = _rms(x_ref[...], g_ref[...]).astype(BF16)

    def proj(c0, n):
        return jnp.dot(hn, w_ref[:, c0:c0 + n].astype(BF16), preferred_element_type=F32)

    @pl.when(i % blocks_per_seq == 0)
    def _():
        gbuf[0:CONV_HALO, :] = jnp.zeros((CONV_HALO, CONV_CH), F32)

    gbuf[CONV_HALO:CONV_HALO + tm, :] = proj(0, CONV_CH) * jax.nn.sigmoid(proj(CONV_CH, CONV_CH))
    span = tm + CONV_HALO - SUBLANES
    for b in range(1, SUBLANES):
        phase[b - 1, 0:span, :] = gbuf[b:b + span, :]
    bias = cb_ref[...]
    lg = lg_ref[...]
    lb = lb_ref[...]

    def conv_block(r0):
        acc = jnp.zeros((CONV_ROWS // SUBLANES, SUBLANES, CONV_CH), F32)
        for j in range(CONV_WIDTH):
            a8, b = divmod(j + CONV_OFF, SUBLANES)
            lo = r0 + a8 * SUBLANES
            if b == 0:
                tap = gbuf[lo:lo + CONV_ROWS, :]
            else:
                tap = phase[b - 1, lo:lo + CONV_ROWS, :]
            acc = acc + tap.reshape(acc.shape) * ck_ref[j][None]
        y = _layer_norm(acc.reshape(CONV_ROWS, CONV_CH) + bias, lg, lb)
        a_ref[r0:r0 + CONV_ROWS, :] = (y * jax.nn.sigmoid(y)).astype(a_ref.dtype)

    cos = cos_ref[...]
    sin = sin_ref[...]
    lane = lax.broadcasted_iota(jnp.int32, (tm, LANES), 1)
    first_half = (lane % HEAD_DIM) < (HEAD_DIM // 2)

    def rope(z):
        rot = jnp.where(first_half,
                        pltpu.roll(z, LANES - HEAD_DIM // 2, 1),
                        pltpu.roll(z, HEAD_DIM // 2, 1))
        return z * cos + rot * sin

    base = 2 * CONV_CH
    for c in range(N_GROUPS):
        zq = proj(base + c * GROUP_WIDTH, GROUP_WIDTH)
        zk = proj(base + ATTN_WIDTH + c * GROUP_WIDTH, GROUP_WIDTH)
        for s in range(GROUP_WIDTH // LANES):
            lo = c * GROUP_WIDTH + s * LANES
            q_ref[:, lo:lo + LANES] = rope(zq[:, s * LANES:(s + 1) * LANES]) * Q_SCALE
            k_ref[:, lo:lo + LANES] = rope(zk[:, s * LANES:(s + 1) * LANES])
        v_ref[:, c * GROUP_WIDTH:(c + 1) * GROUP_WIDTH] = proj(
            base + 2 * ATTN_WIDTH + c * GROUP_WIDTH, GROUP_WIDTH)

    for r0 in range(0, tm, CONV_ROWS):
        conv_block(r0)
    gbuf[0:CONV_HALO, :] = gbuf[tm:tm + CONV_HALO, :]


def _in0_call(x2, g, w_in, layer, cos_t, sin_t, conv_k, conv_b, ln_g, ln_b, seq, tm):
    t, d = x2.shape
    blocks_per_seq = seq // tm
    row = lambda i: (i, 0)
    return pl.pallas_call(
        functools.partial(_in0_kernel, blocks_per_seq=blocks_per_seq),
        grid=(t // tm,),
        in_specs=[
            pl.BlockSpec((tm, d), row),
            _const_spec((1, d)),
            _layer_spec(w_in.shape, layer),
            pl.BlockSpec((tm, LANES), lambda i: (i % blocks_per_seq, 0)),
            pl.BlockSpec((tm, LANES), lambda i: (i % blocks_per_seq, 0)),
            _const_spec((CONV_WIDTH, SUBLANES, CONV_CH)),
            _const_spec((1, CONV_CH)),
            _const_spec((1, CONV_CH)),
            _const_spec((1, CONV_CH)),
        ],
        out_specs=[
            pl.BlockSpec((tm, CONV_CH), row),
            pl.BlockSpec((tm, ATTN_WIDTH), row),
            pl.BlockSpec((tm, ATTN_WIDTH), row),
            pl.BlockSpec((tm, ATTN_WIDTH), row),
        ],
        out_shape=[
            jax.ShapeDtypeStruct((t, CONV_CH), BF16),
            jax.ShapeDtypeStruct((t, ATTN_WIDTH), F32),
            jax.ShapeDtypeStruct((t, ATTN_WIDTH), F32),
            jax.ShapeDtypeStruct((t, ATTN_WIDTH), F32),
        ],
        scratch_shapes=[
            pltpu.VMEM((CONV_HALO + tm, CONV_CH), F32),
            pltpu.VMEM((SUBLANES - 1, CONV_HALO + tm - SUBLANES, CONV_CH), F32),
        ],
        compiler_params=_params(("arbitrary",)),
        name="l0_in_proj",
    )(x2, g, w_in, cos_t, sin_t, conv_k, conv_b, ln_g, ln_b)


ATTN_CLASSES = 4


def _attn_kernel(q0, k0, v0, q1, k1, v1, q2, k2, v2, out_ref, acc_o, acc_m, acc_l, stg_q, stg_k,
                 stg_v):
    seq = out_ref.shape[0]
    f = ATTN_CLASSES
    sub = seq // f
    piece = BAND // f
    lane = lax.broadcasted_iota(jnp.int32, (BAND, LANES), 1)
    head0 = lane < HEAD_DIM

    def ds(start, size, stride):
        return pl.ds(start, size) if stride == 1 else pl.ds(start, size, stride=stride)

    def load(ref, idxs):
        parts = [ref[ix, :] for ix in idxs]
        return parts[0] if len(parts) == 1 else jnp.concatenate(parts, axis=0)

    def store(ref, idxs, val):
        n = val.shape[0] // len(idxs)
        for i, ix in enumerate(idxs):
            ref[ix, :] = val[i * n:(i + 1) * n]

    def biases(pos):
        row2 = pos(lax.broadcasted_iota(jnp.int32, (2 * BAND, 2 * BAND), 0) & (BAND - 1))
        col2 = lax.broadcasted_iota(jnp.int32, (2 * BAND, 2 * BAND), 1)
        key2 = pos(col2 & (BAND - 1))
        allowed = jnp.logical_or(jnp.logical_and(col2 < BAND, key2 >= row2),
                                 jnp.logical_and(col2 >= BAND, key2 <= row2))
        row1 = pos(lax.broadcasted_iota(jnp.int32, (2 * BAND, BAND), 0) & (BAND - 1))
        key1 = pos(lax.broadcasted_iota(jnp.int32, (2 * BAND, BAND), 1))
        return (jnp.where(allowed, 0.0, NEG_BIG).astype(F32),
                jnp.where(key1 <= row1, 0.0, NEG_BIG).astype(F32))

    bias_pc, bias_c = biases(lambda p: p)
    bias_pc_perm, bias_c_perm = biases(lambda p: f * (p & (piece - 1)) + p // piece)

    def attend(q, kb, vb, bias):
        qs = jnp.concatenate([jnp.where(head0, q, 0.0), jnp.where(head0, 0.0, q)],
                             axis=0).astype(BF16)
        s = lax.dot_general(qs, kb, (((1,), (1,)), ((), ())), preferred_element_type=F32) + bias
        m = jnp.max(s, axis=-1, keepdims=True)
        p = jnp.exp2(s - m)
        l = jnp.sum(p, axis=-1, keepdims=True)
        o = jnp.dot(p.astype(BF16), vb, preferred_element_type=F32)
        o = jnp.where(head0, o[:BAND], o[BAND:])
        m = jnp.where(head0, m[:BAND], m[BAND:])
        l = jnp.where(head0, l[:BAND], l[BAND:])
        return o, m, l

    def emit(first, last, acc_idx, out_idx, o, m, l):
        if first:
            store(acc_o, acc_idx, o)
            store(acc_m, acc_idx, m)
            store(acc_l, acc_idx, l)
            return
        m_old = load(acc_m, acc_idx)
        m_new = jnp.maximum(m_old, m)
        a_old = jnp.exp2(m_old - m_new)
        a_new = jnp.exp2(m - m_new)
        o_new = load(acc_o, acc_idx) * a_old + o * a_new
        l_new = load(acc_l, acc_idx) * a_old + l * a_new
        if last:
            store(out_ref, out_idx, o_new / l_new)
        else:
            store(acc_o, acc_idx, o_new)
            store(acc_l, acc_idx, l_new)
            store(acc_m, acc_idx, m_new)

    groups = ((q0, k0, v0), (q1, k1, v1), (q2, k2, v2))
    order = sorted(range(N_GROUPS), key=lambda g: DILATED_PAIRS[g][1])
    order = order[1:2] + order[2:] + order[:1]
    for pos, g in enumerate(order):
        dil = DILATED_PAIRS[g][1]
        q_ref, k_ref, v_ref = groups[g]
        nb = (seq // dil) // BAND
        first, last = pos == 0, pos == N_GROUPS - 1
        if dil == 1:
            classes = [(lambda n: [ds(BAND * n + c, piece, f) for c in range(f)],
                        lambda n: [ds(c * sub + piece * n, piece, 1) for c in range(f)],
                        lambda n: [ds(BAND * n + c, piece, f) for c in range(f)])]
            srcs = (q_ref, k_ref, v_ref)
            b_pc, b_c = bias_pc_perm, bias_c_perm
        else:
            assert dil % f == 0
            e = dil // f
            if e > 1:
                for ref, stg in ((q_ref, stg_q), (k_ref, stg_k), (v_ref, stg_v)):
                    for c in range(f):
                        stg[c * sub:(c + 1) * sub, :] = ref[pl.ds(c, sub, stride=f), :]
                srcs = (stg_q, stg_k, stg_v)
            else:
                srcs = (q_ref, k_ref, v_ref)
            classes = []
            for r in range(dil):
                acc_rows = (lambda n, r=r: [ds((r % f) * sub + r // f + e * BAND * n, BAND, e)])
                nat_rows = (lambda n, r=r: [ds(r + dil * BAND * n, BAND, dil)])
                classes.append((acc_rows if e > 1 else nat_rows, acc_rows, nat_rows))
            b_pc, b_c = bias_pc, bias_c
        for in_rows, acc_rows, out_rows in classes:
            kbs = [load(srcs[1], in_rows(n)).astype(BF16) for n in range(nb)]
            vbs = [load(srcs[2], in_rows(n)).astype(BF16) for n in range(nb)]
            for n in range(nb):
                q = load(srcs[0], in_rows(n))
                if n == 0:
                    o, m, l = attend(q, kbs[0], vbs[0], b_c)
                else:
                    o, m, l = attend(q, jnp.concatenate([kbs[n - 1], kbs[n]], axis=0),
                                     jnp.concatenate([vbs[n - 1], vbs[n]], axis=0), b_pc)
                emit(first, last, acc_rows(n), out_rows(n), o, m, l)


def _attn_call(q3, k3, v3):
    b, s, _ = q3.shape
    pairs = GROUP_WIDTH // LANES
    in_specs = []
    args = []
    for g in range(N_GROUPS):
        for arr in (q3, k3, v3):
            in_specs.append(pl.BlockSpec((None, s, LANES),
                                         lambda bi, p, g=g: (bi, 0, g * pairs + p)))
            args.append(arr)
    return pl.pallas_call(
        _attn_kernel,
        grid=(b, pairs),
        in_specs=in_specs,
        out_specs=pl.BlockSpec((None, s, LANES), lambda bi, p: (bi, 0, p)),
        out_shape=jax.ShapeDtypeStruct((b, s, GROUP_WIDTH), F32),
        scratch_shapes=[pltpu.VMEM((s, LANES), F32)] * 6,
        compiler_params=_params(("parallel", "parallel")),
        name="l0_dilated_attention",
    )(*args)


FFN_CHUNK = 1024


def _ffn_tail(h, g_ref, w1_ref, w2_ref):
    hn = _rms(h, g_ref[...]).astype(BF16)
    d_ff = w1_ref.shape[1]
    acc = h
    for c0 in range(0, d_ff, FFN_CHUNK):
        a = jnp.dot(hn, w1_ref[:, c0:c0 + FFN_CHUNK].astype(BF16), preferred_element_type=F32)
        a = jnp.square(jnp.maximum(a, 0.0)).astype(BF16)
        acc = acc + jnp.dot(a, w2_ref[c0:c0 + FFN_CHUNK, :].astype(BF16),
                            preferred_element_type=F32)
    return acc


def _out0_kernel(x_ref, a_ref, att_ref, wout_ref, g_ref, w1_ref, w2_ref, o_ref):
    c = a_ref.shape[1]
    mix = jnp.dot(a_ref[...].astype(BF16), wout_ref[0:c, :].astype(BF16),
                  preferred_element_type=F32)
    mix = mix + jnp.dot(att_ref[...].astype(BF16), wout_ref[c:, :].astype(BF16),
                        preferred_element_type=F32)
    o_ref[...] = _ffn_tail(x_ref[...] + mix, g_ref, w1_ref, w2_ref)


def _out0_call(x2, a2, att2, w_out, g, w1, w2, layer, tm):
    t, d = x2.shape
    row = lambda i: (i, 0)
    return pl.pallas_call(
        _out0_kernel,
        grid=(t // tm,),
        in_specs=[
            pl.BlockSpec((tm, d), row),
            pl.BlockSpec((tm, a2.shape[1]), row),
            pl.BlockSpec((tm, att2.shape[1]), row),
            _layer_spec(w_out.shape, layer),
            _const_spec((1, d)),
            _layer_spec(w1.shape, 2 * layer),
            _layer_spec(w2.shape, 2 * layer),
        ],
        out_specs=pl.BlockSpec((tm, d), row),
        out_shape=jax.ShapeDtypeStruct((t, d), F32),
        compiler_params=_params(("parallel",)),
        name="l0_out_proj_ffn",
    )(x2, a2, att2, w_out, g, w1, w2)


SCONV_HALO = 8
GELU_C = math.sqrt(2.0 / math.pi)


def _gelu_tanh(x):
    return 0.5 * x * (1.0 + jnp.tanh(GELU_C * (x + 0.044715 * (x * x * x))))


def _l1_kernel(h_ref, g_ref, win_ref, ck_ref, lg_ref, lb_ref, sgw_ref, sgbt_ref, wout_ref,
               fg_ref, w1_ref, w2_ref, final_g_ref, o_ref, ybuf, *, blocks_per_seq):
    tm = h_ref.shape[0]
    i = pl.program_id(0)
    h = h_ref[...]
    hn = _rms(h, g_ref[...]).astype(BF16)

    def proj(c0, n):
        return jnp.dot(hn, win_ref[:, c0:c0 + n].astype(BF16), preferred_element_type=F32)

    gb = proj(0, SCONV_CH)
    y = proj(SCONV_CH, SCONV_CH) * proj(2 * SCONV_CH, SCONV_CH)

    @pl.when(i % blocks_per_seq == 0)
    def _():
        ybuf[0:SCONV_HALO, :] = jnp.zeros((SCONV_HALO, SCONV_CH), F32)

    ybuf[SCONV_HALO:SCONV_HALO + tm, :] = y
    conv = y * ck_ref[SCONV_WIDTH - 1:SCONV_WIDTH, :]
    for j in range(SCONV_WIDTH - 1):
        sh = SCONV_WIDTH - 1 - j
        conv = conv + ybuf[SCONV_HALO - sh:SCONV_HALO - sh + tm, :] * ck_ref[j:j + 1, :]
    ybuf[0:SCONV_HALO, :] = y[tm - SCONV_HALO:tm, :]
    c_out = (gb * conv).astype(BF16)

    u = _gelu_tanh(proj(3 * SCONV_CH, SG_CH))
    v = _gelu_tanh(proj(3 * SCONV_CH + SG_CH, SG_CH))
    v = _layer_norm(v, lg_ref[...], lb_ref[...]).astype(BF16)
    rr = lax.broadcasted_iota(jnp.int32, (CHUNK, CHUNK), 0)
    cc = lax.broadcasted_iota(jnp.int32, (CHUNK, CHUNK), 1)
    causal = rr >= cc
    d_cols = []
    for gi in range(SG_GROUPS):
        ws = jnp.where(causal, sgw_ref[gi], 0.0).astype(BF16)
        bcol = sgbt_ref[:, gi:gi + 1]
        chunks = []
        for n in range(tm // CHUNK):
            vv = v[n * CHUNK:(n + 1) * CHUNK, gi * SG_HEAD:(gi + 1) * SG_HEAD]
            chunks.append(jnp.dot(ws, vv, preferred_element_type=F32) + bcol)
        d_cols.append(jnp.concatenate(chunks, axis=0))
    d_out = (u * jnp.concatenate(d_cols, axis=1)).astype(BF16)

    mix = jnp.dot(c_out, wout_ref[0:SCONV_CH, :].astype(BF16), preferred_element_type=F32)
    mix = mix + jnp.dot(d_out, wout_ref[SCONV_CH:, :].astype(BF16),
                        preferred_element_type=F32)
    o_ref[...] = _rms(_ffn_tail(h + mix, fg_ref, w1_ref, w2_ref), final_g_ref[...])


def _l1_call(h2, g, w_in, conv_k, ln_g, ln_b, sg_w, sg_bt, w_out, layer, ffn_g, w1, w2, ffn_layer,
             final_g, seq, tm):
    t, d = h2.shape
    row = lambda i: (i, 0)
    return pl.pallas_call(
        functools.partial(_l1_kernel, blocks_per_seq=seq // tm),
        grid=(t // tm,),
        in_specs=[
            pl.BlockSpec((tm, d), row),
            _const_spec((1, d)),
            _layer_spec(w_in.shape, layer),
            _const_spec(conv_k.shape),
            _const_spec((1, SG_CH)),
            _const_spec((1, SG_CH)),
            _const_spec(sg_w.shape),
            _const_spec(sg_bt.shape),
            _layer_spec(w_out.shape, layer),
            _const_spec((1, d)),
            _layer_spec(w1.shape, ffn_layer),
            _layer_spec(w2.shape, ffn_layer),
            _const_spec((1, d)),
        ],
        out_specs=pl.BlockSpec((tm, d), row),
        out_shape=jax.ShapeDtypeStruct((t, d), F32),
        scratch_shapes=[pltpu.VMEM((SCONV_HALO + tm, SCONV_CH), F32)],
        compiler_params=_params(("arbitrary",)),
        name="l1_mixer_ffn_final",
    )(h2, g, w_in, conv_k, ln_g, ln_b, sg_w, sg_bt, w_out, ffn_g, w1, w2, final_g)


def _rope_tables(seq):
    half = HEAD_DIM // 2
    inv = ROPE_THETA ** (-jnp.arange(half, dtype=F32) / half)
    ang = jnp.arange(seq, dtype=F32)[:, None] * inv[None, :]
    cos, sin = jnp.cos(ang), jnp.sin(ang)
    reps = LANES // HEAD_DIM
    cos_t = jnp.tile(jnp.concatenate([cos, cos], axis=-1), (1, reps))
    sin_t = jnp.tile(jnp.concatenate([-sin, sin], axis=-1), (1, reps))
    return cos_t, sin_t


def kernel(x, norm_mix_g, norm_ffn_g, even_w_in, even_conv_k, even_conv_b, even_ln_g, even_ln_b,
           even_w_out, odd_w_in, odd_conv_k, odd_ln_g, odd_ln_b, odd_sg_w, odd_sg_b, odd_w_out,
           ffn_w1, ffn_w2, final_g):
    b, s, d = x.shape
    t = b * s
    x2 = x.reshape(t, d)
    cos_t, sin_t = _rope_tables(s)

    a, q, k, v = _in0_call(x2, norm_mix_g[0:1], even_w_in, 0, cos_t, sin_t,
                           jnp.broadcast_to(even_conv_k[0][:, None, :],
                                            (CONV_WIDTH, SUBLANES, CONV_CH)),
                           even_conv_b[0:1], even_ln_g[0:1], even_ln_b[0:1], s, ROWS_IN_PROJ)
    att = _attn_call(q.reshape(b, s, ATTN_WIDTH), k.reshape(b, s, ATTN_WIDTH),
                     v.reshape(b, s, ATTN_WIDTH))
    h = _out0_call(x2, a, att.reshape(t, GROUP_WIDTH), even_w_out, norm_ffn_g[0:1],
                   ffn_w1, ffn_w2, 0, ROWS_TAIL)
    h = _l1_call(h, norm_mix_g[1:2], odd_w_in, odd_conv_k[0], odd_ln_g[0:1], odd_ln_b[0:1],
                 odd_sg_w[0], odd_sg_b[0].T, odd_w_out.astype(BF16), 0, norm_ffn_g[1:2], ffn_w1, ffn_w2, 1,
                 final_g[None, :], s, ROWS_TAIL)
    return h.reshape(b, s, d)
```

```python
import functools
import math

import jax
import jax.numpy as jnp
from jax import lax
from jax.experimental import pallas as pl
from jax.experimental.pallas import tpu as pltpu

F32 = jnp.float32
BF16 = jnp.bfloat16

EPS = 1e-6
LANES = 128
CONV_CH = 512
CONV_WIDTH = 31
HEAD_DIM = 64
HEADS_PER_GROUP = 8
DILATED_PAIRS = ((128, 1), (512, 4), (2048, 16))
N_GROUPS = len(DILATED_PAIRS)
GROUP_WIDTH = HEADS_PER_GROUP * HEAD_DIM
ATTN_WIDTH = N_GROUPS * GROUP_WIDTH
ROPE_THETA = 10000.0
SCONV_CH = 512
SCONV_WIDTH = 3
SG_GROUPS = 4
SG_HEAD = 128
SG_CH = SG_GROUPS * SG_HEAD
CHUNK = 128
BAND = 128
NEG_BIG = -1e30
Q_SCALE = HEAD_DIM ** -0.5 * math.log2(math.e)

VMEM_LIMIT = 60 * 1024 * 1024
ROWS_IN_PROJ = 512
ROWS_TAIL = 512


def _rms(x, g):
    return x * lax.rsqrt(jnp.mean(x * x, axis=-1, keepdims=True) + EPS) * g


def _layer_norm(x, g, b):
    mu = jnp.mean(x, axis=-1, keepdims=True)
    xc = x - mu
    return xc * lax.rsqrt(jnp.mean(xc * xc, axis=-1, keepdims=True) + EPS) * g + b


def _const_spec(shape):
    return pl.BlockSpec(shape, lambda *_: (0,) * len(shape), pipeline_mode=pl.Buffered(1))


def _layer_spec(stacked_shape, layer):
    return pl.BlockSpec((None,) + tuple(stacked_shape[1:]), lambda *_: (layer, 0, 0),
                        pipeline_mode=pl.Buffered(1))


def _params(sem):
    return pltpu.CompilerParams(dimension_semantics=sem, vmem_limit_bytes=VMEM_LIMIT)


SUBLANES = 8
CONV_HALO = 32
CONV_ROWS = 64
CONV_OFF = CONV_HALO - (CONV_WIDTH - 1)


def _in0_kernel(x_ref, g_ref, w_ref, cos_ref, sin_ref, ck_ref, cb_ref, lg_ref, lb_ref,
                a_ref, q_ref, k_ref, v_ref, gbuf, phase, *, blocks_per_seq):
    tm = x_ref.shape[0]
    i = pl.program_id(0)
    hn = _rms(x_ref[...], g_ref[...]).astype(BF16)

    def proj(c0, n):
        return jnp.dot(hn, w_ref[:, c0:c0 + n].astype(BF16), preferred_element_type=F32)

    @pl.when(i % blocks_per_seq == 0)
    def _():
        gbuf[0:CONV_HALO, :] = jnp.zeros((CONV_HALO, CONV_CH), F32)

    gbuf[CONV_HALO:CONV_HALO + tm, :] = proj(0, CONV_CH) * jax.nn.sigmoid(proj(CONV_CH, CONV_CH))
    span = tm + CONV_HALO - SUBLANES
    for b in range(1, SUBLANES):
        phase[b - 1, 0:span, :] = gbuf[b:b + span, :]
    bias = cb_ref[...]
    lg = lg_ref[...]
    lb = lb_ref[...]

    def conv_block(r0, gate):
        bits = pltpu.bitcast(gate[0:SUBLANES, 0:LANES], jnp.uint32)
        zero = pltpu.bitcast((bits >> 16) >> 16, F32)
        acc = jnp.broadcast_to(jnp.tile(zero, (1, CONV_CH // LANES))[None],
                               (CONV_ROWS // SUBLANES, SUBLANES, CONV_CH))
        for j in range(CONV_WIDTH):
            a8, b = divmod(j + CONV_OFF, SUBLANES)
            lo = r0 + a8 * SUBLANES
            if b == 0:
                tap = gbuf[lo:lo + CONV_ROWS, :]
            else:
                tap = phase[b - 1, lo:lo + CONV_ROWS, :]
            acc = acc + tap.reshape(acc.shape) * ck_ref[j][None]
        y = _layer_norm(acc.reshape(CONV_ROWS, CONV_CH) + bias, lg, lb)
        a_ref[r0:r0 + CONV_ROWS, :] = (y * jax.nn.sigmoid(y)).astype(a_ref.dtype)

    cos = cos_ref[...]
    sin = sin_ref[...]
    lane = lax.broadcasted_iota(jnp.int32, (tm, LANES), 1)
    first_half = (lane % HEAD_DIM) < (HEAD_DIM // 2)

    def rope(z):
        rot = jnp.where(first_half,
                        pltpu.roll(z, LANES - HEAD_DIM // 2, 1),
                        pltpu.roll(z, HEAD_DIM // 2, 1))
        return z * cos + rot * sin

    base = 2 * CONV_CH
    chunks = []
    for c in range(N_GROUPS):
        zq = proj(base + c * GROUP_WIDTH, GROUP_WIDTH)
        zk = proj(base + ATTN_WIDTH + c * GROUP_WIDTH, GROUP_WIDTH)
        for s in range(GROUP_WIDTH // LANES):
            lo = c * GROUP_WIDTH + s * LANES
            q_ref[:, lo:lo + LANES] = rope(zq[:, s * LANES:(s + 1) * LANES]) * Q_SCALE
            k_ref[:, lo:lo + LANES] = rope(zk[:, s * LANES:(s + 1) * LANES])
        zv = proj(base + 2 * ATTN_WIDTH + c * GROUP_WIDTH, GROUP_WIDTH)
        v_ref[:, c * GROUP_WIDTH:(c + 1) * GROUP_WIDTH] = zv
        chunks += [zq, zk, zv]

    starts = list(range(0, tm, CONV_ROWS))
    for n, r0 in enumerate(starts):
        conv_block(r0, chunks[n * len(chunks) // len(starts)])
    gbuf[0:CONV_HALO, :] = gbuf[tm:tm + CONV_HALO, :]


def _in0_call(x2, g, w_in, layer, cos_t, sin_t, conv_k, conv_b, ln_g, ln_b, seq, tm):
    t, d = x2.shape
    blocks_per_seq = seq // tm
    row = lambda i: (i, 0)
    return pl.pallas_call(
        functools.partial(_in0_kernel, blocks_per_seq=blocks_per_seq),
        grid=(t // tm,),
        in_specs=[
            pl.BlockSpec((tm, d), row),
            _const_spec((1, d)),
            _layer_spec(w_in.shape, layer),
            pl.BlockSpec((tm, LANES), lambda i: (i % blocks_per_seq, 0)),
            pl.BlockSpec((tm, LANES), lambda i: (i % blocks_per_seq, 0)),
            _const_spec((CONV_WIDTH, SUBLANES, CONV_CH)),
            _const_spec((1, CONV_CH)),
            _const_spec((1, CONV_CH)),
            _const_spec((1, CONV_CH)),
        ],
        out_specs=[
            pl.BlockSpec((tm, CONV_CH), row),
            pl.BlockSpec((tm, ATTN_WIDTH), row),
            pl.BlockSpec((tm, ATTN_WIDTH), row),
            pl.BlockSpec((tm, ATTN_WIDTH), row),
        ],
        out_shape=[
            jax.ShapeDtypeStruct((t, CONV_CH), BF16),
            jax.ShapeDtypeStruct((t, ATTN_WIDTH), F32),
            jax.ShapeDtypeStruct((t, ATTN_WIDTH), F32),
            jax.ShapeDtypeStruct((t, ATTN_WIDTH), F32),
        ],
        scratch_shapes=[
            pltpu.VMEM((CONV_HALO + tm, CONV_CH), F32),
            pltpu.VMEM((SUBLANES - 1, CONV_HALO + tm - SUBLANES, CONV_CH), F32),
        ],
        compiler_params=_params(("arbitrary",)),
        name="l0_in_proj",
    )(x2, g, w_in, cos_t, sin_t, conv_k, conv_b, ln_g, ln_b)


ATTN_CLASSES = 4


def _attn_kernel(q0, k0, v0, q1, k1, v1, q2, k2, v2, out_ref, acc_o, acc_m, acc_l, stg_q, stg_k,
                 stg_v):
    seq = out_ref.shape[0]
    f = ATTN_CLASSES
    sub = seq // f
    piece = BAND // f
    lane = lax.broadcasted_iota(jnp.int32, (BAND, LANES), 1)
    head0 = lane < HEAD_DIM

    def ds(start, size, stride):
        return pl.ds(start, size) if stride == 1 else pl.ds(start, size, stride=stride)

    def load(ref, idxs):
        parts = [ref[ix, :] for ix in idxs]
        return parts[0] if len(parts) == 1 else jnp.concatenate(parts, axis=0)

    def store(ref, idxs, val):
        n = val.shape[0] // len(idxs)
        for i, ix in enumerate(idxs):
            ref[ix, :] = val[i * n:(i + 1) * n]

    def biases(pos):
        row2 = pos(lax.broadcasted_iota(jnp.int32, (2 * BAND, 2 * BAND), 0) & (BAND - 1))
        col2 = lax.broadcasted_iota(jnp.int32, (2 * BAND, 2 * BAND), 1)
        key2 = pos(col2 & (BAND - 1))
        allowed = jnp.logical_or(jnp.logical_and(col2 < BAND, key2 >= row2),
                                 jnp.logical_and(col2 >= BAND, key2 <= row2))
        row1 = pos(lax.broadcasted_iota(jnp.int32, (2 * BAND, BAND), 0) & (BAND - 1))
        key1 = pos(lax.broadcasted_iota(jnp.int32, (2 * BAND, BAND), 1))
        return (jnp.where(allowed, 0.0, NEG_BIG).astype(F32),
                jnp.where(key1 <= row1, 0.0, NEG_BIG).astype(F32))

    bias_pc, bias_c = biases(lambda p: p)
    bias_pc_perm, bias_c_perm = biases(lambda p: f * (p & (piece - 1)) + p // piece)

    def attend(q, kb, vb, bias):
        qs = jnp.concatenate([jnp.where(head0, q, 0.0), jnp.where(head0, 0.0, q)],
                             axis=0).astype(BF16)
        s = lax.dot_general(qs, kb, (((1,), (1,)), ((), ())), preferred_element_type=F32) + bias
        m = jnp.max(s, axis=-1, keepdims=True)
        p = jnp.exp2(s - m)
        l = jnp.sum(p, axis=-1, keepdims=True)
        o = jnp.dot(p.astype(BF16), vb, preferred_element_type=F32)
        o = jnp.where(head0, o[:BAND], o[BAND:])
        m = jnp.where(head0, m[:BAND], m[BAND:])
        l = jnp.where(head0, l[:BAND], l[BAND:])
        return o, m, l

    def emit(first, last, acc_idx, out_idx, o, m, l):
        if first:
            store(acc_o, acc_idx, o)
            store(acc_m, acc_idx, m)
            store(acc_l, acc_idx, l)
            return
        m_old = load(acc_m, acc_idx)
        m_new = jnp.maximum(m_old, m)
        a_old = jnp.exp2(m_old - m_new)
        a_new = jnp.exp2(m - m_new)
        o_new = load(acc_o, acc_idx) * a_old + o * a_new
        l_new = load(acc_l, acc_idx) * a_old + l * a_new
        if last:
            store(out_ref, out_idx, o_new / l_new)
        else:
            store(acc_o, acc_idx, o_new)
            store(acc_l, acc_idx, l_new)
            store(acc_m, acc_idx, m_new)

    groups = ((q0, k0, v0), (q1, k1, v1), (q2, k2, v2))
    order = sorted(range(N_GROUPS), key=lambda g: DILATED_PAIRS[g][1])
    order = order[1:2] + order[2:] + order[:1]
    for pos, g in enumerate(order):
        dil = DILATED_PAIRS[g][1]
        q_ref, k_ref, v_ref = groups[g]
        nb = (seq // dil) // BAND
        first, last = pos == 0, pos == N_GROUPS - 1
        if dil == 1:
            classes = [(lambda n: [ds(BAND * n + c, piece, f) for c in range(f)],
                        lambda n: [ds(c * sub + piece * n, piece, 1) for c in range(f)],
                        lambda n: [ds(BAND * n + c, piece, f) for c in range(f)])]
            srcs = (q_ref, k_ref, v_ref)
            b_pc, b_c = bias_pc_perm, bias_c_perm
        else:
            assert dil % f == 0
            e = dil // f
            if e > 1:
                for ref, stg in ((q_ref, stg_q), (k_ref, stg_k), (v_ref, stg_v)):
                    for c in range(f):
                        stg[c * sub:(c + 1) * sub, :] = ref[pl.ds(c, sub, stride=f), :]
                srcs = (stg_q, stg_k, stg_v)
            else:
                srcs = (q_ref, k_ref, v_ref)
            classes = []
            for r in range(dil):
                acc_rows = (lambda n, r=r: [ds((r % f) * sub + r // f + e * BAND * n, BAND, e)])
                nat_rows = (lambda n, r=r: [ds(r + dil * BAND * n, BAND, dil)])
                classes.append((acc_rows if e > 1 else nat_rows, acc_rows, nat_rows))
            b_pc, b_c = bias_pc, bias_c
        for in_rows, acc_rows, out_rows in classes:
            kbs = [load(srcs[1], in_rows(n)).astype(BF16) for n in range(nb)]
            vbs = [load(srcs[2], in_rows(n)).astype(BF16) for n in range(nb)]
            for n in range(nb):
                q = load(srcs[0], in_rows(n))
                if n == 0:
                    o, m, l = attend(q, kbs[0], vbs[0], b_c)
                else:
                    o, m, l = attend(q, jnp.concatenate([kbs[n - 1], kbs[n]], axis=0),
                                     jnp.concatenate([vbs[n - 1], vbs[n]], axis=0), b_pc)
                emit(first, last, acc_rows(n), out_rows(n), o, m, l)


def _attn_call(q3, k3, v3):
    b, s, _ = q3.shape
    pairs = GROUP_WIDTH // LANES
    in_specs = []
    args = []
    for g in range(N_GROUPS):
        for arr in (q3, k3, v3):
            in_specs.append(pl.BlockSpec((None, s, LANES),
                                         lambda bi, p, g=g: (bi, 0, g * pairs + p)))
            args.append(arr)
    return pl.pallas_call(
        _attn_kernel,
        grid=(b, pairs),
        in_specs=in_specs,
        out_specs=pl.BlockSpec((None, s, LANES), lambda bi, p: (bi, 0, p)),
        out_shape=jax.ShapeDtypeStruct((b, s, GROUP_WIDTH), F32),
        scratch_shapes=[pltpu.VMEM((s, LANES), F32)] * 6,
        compiler_params=_params(("parallel", "parallel")),
        name="l0_dilated_attention",
    )(*args)


FFN_CHUNK = 1024


def _ffn_tail(h, g_ref, w1_ref, w2_ref):
    hn = _rms(h, g_ref[...]).astype(BF16)
    d_ff = w1_ref.shape[1]
    acc = h
    for c0 in range(0, d_ff, FFN_CHUNK):
        a = jnp.dot(hn, w1_ref[:, c0:c0 + FFN_CHUNK].astype(BF16), preferred_element_type=F32)
        a = jnp.square(jnp.maximum(a, 0.0)).astype(BF16)
        acc = acc + jnp.dot(a, w2_ref[c0:c0 + FFN_CHUNK, :].astype(BF16),
                            preferred_element_type=F32)
    return acc


def _out0_kernel(x_ref, a_ref, att_ref, wout_ref, g_ref, w1_ref, w2_ref, o_ref):
    c = a_ref.shape[1]
    mix = jnp.dot(a_ref[...].astype(BF16), wout_ref[0:c, :].astype(BF16),
                  preferred_element_type=F32)
    mix = mix + jnp.dot(att_ref[...].astype(BF16), wout_ref[c:, :].astype(BF16),
                        preferred_element_type=F32)
    o_ref[...] = _ffn_tail(x_ref[...] + mix, g_ref, w1_ref, w2_ref)


def _out0_call(x2, a2, att2, w_out, g, w1, w2, layer, tm):
    t, d = x2.shape
    row = lambda i: (i, 0)
    return pl.pallas_call(
        _out0_kernel,
        grid=(t // tm,),
        in_specs=[
            pl.BlockSpec((tm, d), row),
            pl.BlockSpec((tm, a2.shape[1]), row),
            pl.BlockSpec((tm, att2.shape[1]), row),
            _layer_spec(w_out.shape, layer),
            _const_spec((1, d)),
            _layer_spec(w1.shape, 2 * layer),
            _layer_spec(w2.shape, 2 * layer),
        ],
        out_specs=pl.BlockSpec((tm, d), row),
        out_shape=jax.ShapeDtypeStruct((t, d), F32),
        compiler_params=_params(("parallel",)),
        name="l0_out_proj_ffn",
    )(x2, a2, att2, w_out, g, w1, w2)


SCONV_HALO = 8
GELU_C = math.sqrt(2.0 / math.pi)


def _gelu_tanh(x):
    return 0.5 * x * (1.0 + jnp.tanh(GELU_C * (x + 0.044715 * (x * x * x))))


def _l1_kernel(h_ref, g_ref, win_ref, ck_ref, lg_ref, lb_ref, sgw_ref, sgbt_ref, wout_ref,
               fg_ref, w1_ref, w2_ref, final_g_ref, o_ref, ybuf, *, blocks_per_seq):
    tm = h_ref.shape[0]
    i = pl.program_id(0)
    h = h_ref[...]
    hn = _rms(h, g_ref[...]).astype(BF16)

    def proj(c0, n):
        return jnp.dot(hn, win_ref[:, c0:c0 + n].astype(BF16), preferred_element_type=F32)

    gb = proj(0, SCONV_CH)
    y = proj(SCONV_CH, SCONV_CH) * proj(2 * SCONV_CH, SCONV_CH)

    @pl.when(i % blocks_per_seq == 0)
    def _():
        ybuf[0:SCONV_HALO, :] = jnp.zeros((SCONV_HALO, SCONV_CH), F32)

    ybuf[SCONV_HALO:SCONV_HALO + tm, :] = y
    conv = y * ck_ref[SCONV_WIDTH - 1:SCONV_WIDTH, :]
    for j in range(SCONV_WIDTH - 1):
        sh = SCONV_WIDTH - 1 - j
        conv = conv + ybuf[SCONV_HALO - sh:SCONV_HALO - sh + tm, :] * ck_ref[j:j + 1, :]
    ybuf[0:SCONV_HALO, :] = y[tm - SCONV_HALO:tm, :]
    c_out = (gb * conv).astype(BF16)

    u = _gelu_tanh(proj(3 * SCONV_CH, SG_CH))
    v = _gelu_tanh(proj(3 * SCONV_CH + SG_CH, SG_CH))
    v = _layer_norm(v, lg_ref[...], lb_ref[...]).astype(BF16)
    rr = lax.broadcasted_iota(jnp.int32, (CHUNK, CHUNK), 0)
    cc = lax.broadcasted_iota(jnp.int32, (CHUNK, CHUNK), 1)
    causal = rr >= cc
    d_cols = []
    for gi in range(SG_GROUPS):
        ws = jnp.where(causal, sgw_ref[gi], 0.0).astype(BF16)
        bcol = sgbt_ref[:, gi:gi + 1]
        chunks = []
        for n in range(tm // CHUNK):
            vv = v[n * CHUNK:(n + 1) * CHUNK, gi * SG_HEAD:(gi + 1) * SG_HEAD]
            chunks.append(jnp.dot(ws, vv, preferred_element_type=F32) + bcol)
        d_cols.append(jnp.concatenate(chunks, axis=0))
    d_out = (u * jnp.concatenate(d_cols, axis=1)).astype(BF16)

    mix = jnp.dot(c_out, wout_ref[0:SCONV_CH, :].astype(BF16), preferred_element_type=F32)
    mix = mix + jnp.dot(d_out, wout_ref[SCONV_CH:, :].astype(BF16),
                        preferred_element_type=F32)
    o_ref[...] = _rms(_ffn_tail(h + mix, fg_ref, w1_ref, w2_ref), final_g_ref[...])


def _l1_call(h2, g, w_in, conv_k, ln_g, ln_b, sg_w, sg_bt, w_out, layer, ffn_g, w1, w2, ffn_layer,
             final_g, seq, tm):
    t, d = h2.shape
    row = lambda i: (i, 0)
    return pl.pallas_call(
        functools.partial(_l1_kernel, blocks_per_seq=seq // tm),
        grid=(t // tm,),
        in_specs=[
            pl.BlockSpec((tm, d), row),
            _const_spec((1, d)),
            _layer_spec(w_in.shape, layer),
            _const_spec(conv_k.shape),
            _const_spec((1, SG_CH)),
            _const_spec((1, SG_CH)),
            _const_spec(sg_w.shape),
            _const_spec(sg_bt.shape),
            _layer_spec(w_out.shape, layer),
            _const_spec((1, d)),
            _layer_spec(w1.shape, ffn_layer),
            _layer_spec(w2.shape, ffn_layer),
            _const_spec((1, d)),
        ],
        out_specs=pl.BlockSpec((tm, d), row),
        out_shape=jax.ShapeDtypeStruct((t, d), F32),
        scratch_shapes=[pltpu.VMEM((SCONV_HALO + tm, SCONV_CH), F32)],
        compiler_params=_params(("arbitrary",)),
        name="l1_mixer_ffn_final",
    )(h2, g, w_in, conv_k, ln_g, ln_b, sg_w, sg_bt, w_out, ffn_g, w1, w2, final_g)


def _rope_tables(seq):
    half = HEAD_DIM // 2
    inv = ROPE_THETA ** (-jnp.arange(half, dtype=F32) / half)
    ang = jnp.arange(seq, dtype=F32)[:, None] * inv[None, :]
    cos, sin = jnp.cos(ang), jnp.sin(ang)
    reps = LANES // HEAD_DIM
    cos_t = jnp.tile(jnp.concatenate([cos, cos], axis=-1), (1, reps))
    sin_t = jnp.tile(jnp.concatenate([-sin, sin], axis=-1), (1, reps))
    return cos_t, sin_t


def kernel(x, norm_mix_g, norm_ffn_g, even_w_in, even_conv_k, even_conv_b, even_ln_g, even_ln_b,
           even_w_out, odd_w_in, odd_conv_k, odd_ln_g, odd_ln_b, odd_sg_w, odd_sg_b, odd_w_out,
           ffn_w1, ffn_w2, final_g):
    b, s, d = x.shape
    t = b * s
    x2 = x.reshape(t, d)
    cos_t, sin_t = _rope_tables(s)

    a, q, k, v = _in0_call(x2, norm_mix_g[0:1], even_w_in, 0, cos_t, sin_t,
                           jnp.broadcast_to(even_conv_k[0][:, None, :],
                                            (CONV_WIDTH, SUBLANES, CONV_CH)),
                           even_conv_b[0:1], even_ln_g[0:1], even_ln_b[0:1], s, ROWS_IN_PROJ)
    att = _attn_call(q.reshape(b, s, ATTN_WIDTH), k.reshape(b, s, ATTN_WIDTH),
                     v.reshape(b, s, ATTN_WIDTH))
    h = _out0_call(x2, a, att.reshape(t, GROUP_WIDTH), even_w_out, norm_ffn_g[0:1],
                   ffn_w1, ffn_w2, 0, ROWS_TAIL)
    h = _l1_call(h, norm_mix_g[1:2], odd_w_in, odd_conv_k[0], odd_ln_g[0:1], odd_ln_b[0:1],
                 odd_sg_w[0], odd_sg_b[0].T, odd_w_out.astype(BF16), 0, norm_ffn_g[1:2], ffn_w1, ffn_w2, 1,
                 final_g[None, :], s, ROWS_TAIL)
    return h.reshape(b, s, d)
```

```python
import functools
import math

import jax
import jax.numpy as jnp
from jax import lax
from jax.experimental import pallas as pl
from jax.experimental.pallas import tpu as pltpu

F32 = jnp.float32
BF16 = jnp.bfloat16

EPS = 1e-6
LANES = 128
CONV_CH = 512
CONV_WIDTH = 31
HEAD_DIM = 64
HEADS_PER_GROUP = 8
DILATED_PAIRS = ((128, 1), (512, 4), (2048, 16))
N_GROUPS = len(DILATED_PAIRS)
GROUP_WIDTH = HEADS_PER_GROUP * HEAD_DIM
ATTN_WIDTH = N_GROUPS * GROUP_WIDTH
ROPE_THETA = 10000.0
SCONV_CH = 512
SCONV_WIDTH = 3
SG_GROUPS = 4
SG_HEAD = 128
SG_CH = SG_GROUPS * SG_HEAD
CHUNK = 128
BAND = 128
NEG_BIG = -1e30
Q_SCALE = HEAD_DIM ** -0.5 * math.log2(math.e)

VMEM_LIMIT = 60 * 1024 * 1024
ROWS_IN_PROJ = 512
ROWS_TAIL = 512


def _rms(x, g):
    return x * lax.rsqrt(jnp.mean(x * x, axis=-1, keepdims=True) + EPS) * g


def _layer_norm(x, g, b):
    mu = jnp.mean(x, axis=-1, keepdims=True)
    xc = x - mu
    return xc * lax.rsqrt(jnp.mean(xc * xc, axis=-1, keepdims=True) + EPS) * g + b


def _const_spec(shape):
    return pl.BlockSpec(shape, lambda *_: (0,) * len(shape), pipeline_mode=pl.Buffered(1))


def _layer_spec(stacked_shape, layer):
    return pl.BlockSpec((None,) + tuple(stacked_shape[1:]), lambda *_: (layer, 0, 0),
                        pipeline_mode=pl.Buffered(1))


def _params(sem):
    return pltpu.CompilerParams(dimension_semantics=sem, vmem_limit_bytes=VMEM_LIMIT)


SUBLANES = 8
CONV_HALO = 32
CONV_ROWS = 64
CONV_OFF = CONV_HALO - (CONV_WIDTH - 1)


def _in0_kernel(x_ref, g_ref, w_ref, cos_ref, sin_ref, ck_ref, cb_ref, lg_ref, lb_ref,
                a_ref, q_ref, k_ref, v_ref, gbuf, phase, *, blocks_per_seq):
    tm = x_ref.shape[0]
    i = pl.program_id(0)
    hn = _rms(x_ref[...], g_ref[...]).astype(BF16)

    def proj(c0, n):
        return jnp.dot(hn, w_ref[:, c0:c0 + n].astype(BF16), preferred_element_type=F32)

    @pl.when(i % blocks_per_seq == 0)
    def _():
        gbuf[0:CONV_HALO, :] = jnp.zeros((CONV_HALO, CONV_CH), F32)

    gbuf[CONV_HALO:CONV_HALO + tm, :] = proj(0, CONV_CH) * jax.nn.sigmoid(proj(CONV_CH, CONV_CH))
    span = tm + CONV_HALO - SUBLANES
    for b in range(1, SUBLANES):
        phase[b - 1, 0:span, :] = gbuf[b:b + span, :]
    bias = cb_ref[...]
    lg = lg_ref[...]
    lb = lb_ref[...]

    def conv_block(r0):
        acc = jnp.zeros((CONV_ROWS // SUBLANES, SUBLANES, CONV_CH), F32)
        for j in range(CONV_WIDTH):
            a8, b = divmod(j + CONV_OFF, SUBLANES)
            lo = r0 + a8 * SUBLANES
            if b == 0:
                tap = gbuf[lo:lo + CONV_ROWS, :]
            else:
                tap = phase[b - 1, lo:lo + CONV_ROWS, :]
            acc = acc + tap.reshape(acc.shape) * ck_ref[j][None]
        y = _layer_norm(acc.reshape(CONV_ROWS, CONV_CH) + bias, lg, lb)
        a_ref[r0:r0 + CONV_ROWS, :] = (y * jax.nn.sigmoid(y)).astype(a_ref.dtype)

    cos = cos_ref[...]
    sin = sin_ref[...]
    lane = lax.broadcasted_iota(jnp.int32, (tm, LANES), 1)
    first_half = (lane % HEAD_DIM) < (HEAD_DIM // 2)

    def rope(z):
        rot = jnp.where(first_half,
                        pltpu.roll(z, LANES - HEAD_DIM // 2, 1),
                        pltpu.roll(z, HEAD_DIM // 2, 1))
        return z * cos + rot * sin

    base = 2 * CONV_CH
    for c in range(N_GROUPS):
        zq = proj(base + c * GROUP_WIDTH, GROUP_WIDTH)
        zk = proj(base + ATTN_WIDTH + c * GROUP_WIDTH, GROUP_WIDTH)
        for s in range(GROUP_WIDTH // LANES):
            lo = c * GROUP_WIDTH + s * LANES
            q_ref[:, lo:lo + LANES] = rope(zq[:, s * LANES:(s + 1) * LANES]) * Q_SCALE
            k_ref[:, lo:lo + LANES] = rope(zk[:, s * LANES:(s + 1) * LANES])
        v_ref[:, c * GROUP_WIDTH:(c + 1) * GROUP_WIDTH] = proj(
            base + 2 * ATTN_WIDTH + c * GROUP_WIDTH, GROUP_WIDTH)

    for r0 in range(0, tm, CONV_ROWS):
        conv_block(r0)
    gbuf[0:CONV_HALO, :] = gbuf[tm:tm + CONV_HALO, :]


def _in0_call(x2, g, w_in, layer, cos_t, sin_t, conv_k, conv_b, ln_g, ln_b, seq, tm):
    t, d = x2.shape
    blocks_per_seq = seq // tm
    row = lambda i: (i, 0)
    return pl.pallas_call(
        functools.partial(_in0_kernel, blocks_per_seq=blocks_per_seq),
        grid=(t // tm,),
        in_specs=[
            pl.BlockSpec((tm, d), row),
            _const_spec((1, d)),
            _layer_spec(w_in.shape, layer),
            pl.BlockSpec((tm, LANES), lambda i: (i % blocks_per_seq, 0)),
            pl.BlockSpec((tm, LANES), lambda i: (i % blocks_per_seq, 0)),
            _const_spec((CONV_WIDTH, SUBLANES, CONV_CH)),
            _const_spec((1, CONV_CH)),
            _const_spec((1, CONV_CH)),
            _const_spec((1, CONV_CH)),
        ],
        out_specs=[
            pl.BlockSpec((tm, CONV_CH), row),
            pl.BlockSpec((tm, ATTN_WIDTH), row),
            pl.BlockSpec((tm, ATTN_WIDTH), row),
            pl.BlockSpec((tm, ATTN_WIDTH), row),
        ],
        out_shape=[
            jax.ShapeDtypeStruct((t, CONV_CH), BF16),
            jax.ShapeDtypeStruct((t, ATTN_WIDTH), F32),
            jax.ShapeDtypeStruct((t, ATTN_WIDTH), F32),
            jax.ShapeDtypeStruct((t, ATTN_WIDTH), F32),
        ],
        scratch_shapes=[
            pltpu.VMEM((CONV_HALO + tm, CONV_CH), F32),
            pltpu.VMEM((SUBLANES - 1, CONV_HALO + tm - SUBLANES, CONV_CH), F32),
        ],
        compiler_params=_params(("arbitrary",)),
        name="l0_in_proj",
    )(x2, g, w_in, cos_t, sin_t, conv_k, conv_b, ln_g, ln_b)


ATTN_CLASSES = 4


def _attn_kernel(q0, k0, v0, q1, k1, v1, q2, k2, v2, bias_pc, bias_c, bias_pc_perm, bias_c_perm,
                 out_ref, acc_o, acc_m, acc_l, stg_q, stg_k, stg_v):
    seq = out_ref.shape[0]
    f = ATTN_CLASSES
    sub = seq // f
    piece = BAND // f
    lane = lax.broadcasted_iota(jnp.int32, (BAND, LANES), 1)
    head0 = lane < HEAD_DIM

    def ds(start, size, stride):
        return pl.ds(start, size) if stride == 1 else pl.ds(start, size, stride=stride)

    def load(ref, idxs):
        parts = [ref[ix, :] for ix in idxs]
        return parts[0] if len(parts) == 1 else jnp.concatenate(parts, axis=0)

    def store(ref, idxs, val):
        n = val.shape[0] // len(idxs)
        for i, ix in enumerate(idxs):
            ref[ix, :] = val[i * n:(i + 1) * n]

    def attend(q, kb, vb, bias):
        qs = jnp.concatenate([jnp.where(head0, q, 0.0), jnp.where(head0, 0.0, q)],
                             axis=0).astype(BF16)
        s = lax.dot_general(qs, kb, (((1,), (1,)), ((), ())),
                            preferred_element_type=F32) + bias[...]
        m = jnp.max(s, axis=-1, keepdims=True)
        p = jnp.exp2(s - m)
        l = jnp.sum(p, axis=-1, keepdims=True)
        o = jnp.dot(p.astype(BF16), vb, preferred_element_type=F32)
        o = jnp.where(head0, o[:BAND], o[BAND:])
        m = jnp.where(head0, m[:BAND], m[BAND:])
        l = jnp.where(head0, l[:BAND], l[BAND:])
        return o, m, l

    def emit(first, last, acc_idx, out_idx, o, m, l):
        if first:
            store(acc_o, acc_idx, o)
            store(acc_m, acc_idx, m)
            store(acc_l, acc_idx, l)
            return
        m_old = load(acc_m, acc_idx)
        m_new = jnp.maximum(m_old, m)
        a_old = jnp.exp2(m_old - m_new)
        a_new = jnp.exp2(m - m_new)
        o_new = load(acc_o, acc_idx) * a_old + o * a_new
        l_new = load(acc_l, acc_idx) * a_old + l * a_new
        if last:
            store(out_ref, out_idx, o_new / l_new)
        else:
            store(acc_o, acc_idx, o_new)
            store(acc_l, acc_idx, l_new)
            store(acc_m, acc_idx, m_new)

    groups = ((q0, k0, v0), (q1, k1, v1), (q2, k2, v2))
    order = sorted(range(N_GROUPS), key=lambda g: DILATED_PAIRS[g][1])
    order = order[1:2] + order[2:] + order[:1]
    for pos, g in enumerate(order):
        dil = DILATED_PAIRS[g][1]
        q_ref, k_ref, v_ref = groups[g]
        nb = (seq // dil) // BAND
        first, last = pos == 0, pos == N_GROUPS - 1
        if dil == 1:
            classes = [(lambda n: [ds(BAND * n + c, piece, f) for c in range(f)],
                        lambda n: [ds(c * sub + piece * n, piece, 1) for c in range(f)],
                        lambda n: [ds(BAND * n + c, piece, f) for c in range(f)])]
            srcs = (q_ref, k_ref, v_ref)
            b_pc, b_c = bias_pc_perm, bias_c_perm
        else:
            assert dil % f == 0
            e = dil // f
            if e > 1:
                for ref, stg in ((q_ref, stg_q), (k_ref, stg_k), (v_ref, stg_v)):
                    for c in range(f):
                        stg[c * sub:(c + 1) * sub, :] = ref[pl.ds(c, sub, stride=f), :]
                srcs = (stg_q, stg_k, stg_v)
            else:
                srcs = (q_ref, k_ref, v_ref)
            classes = []
            for r in range(dil):
                acc_rows = (lambda n, r=r: [ds((r % f) * sub + r // f + e * BAND * n, BAND, e)])
                nat_rows = (lambda n, r=r: [ds(r + dil * BAND * n, BAND, dil)])
                classes.append((acc_rows if e > 1 else nat_rows, acc_rows, nat_rows))
            b_pc, b_c = bias_pc, bias_c
        for in_rows, acc_rows, out_rows in classes:
            kbs = [load(srcs[1], in_rows(n)).astype(BF16) for n in range(nb)]
            vbs = [load(srcs[2], in_rows(n)).astype(BF16) for n in range(nb)]
            for n in range(nb):
                q = load(srcs[0], in_rows(n))
                if n == 0:
                    o, m, l = attend(q, kbs[0], vbs[0], b_c)
                else:
                    o, m, l = attend(q, jnp.concatenate([kbs[n - 1], kbs[n]], axis=0),
                                     jnp.concatenate([vbs[n - 1], vbs[n]], axis=0), b_pc)
                emit(first, last, acc_rows(n), out_rows(n), o, m, l)


def _attn_biases(pos):
    row2 = pos(lax.broadcasted_iota(jnp.int32, (2 * BAND, 2 * BAND), 0) & (BAND - 1))
    col2 = lax.broadcasted_iota(jnp.int32, (2 * BAND, 2 * BAND), 1)
    key2 = pos(col2 & (BAND - 1))
    allowed = jnp.logical_or(jnp.logical_and(col2 < BAND, key2 >= row2),
                             jnp.logical_and(col2 >= BAND, key2 <= row2))
    row1 = pos(lax.broadcasted_iota(jnp.int32, (2 * BAND, BAND), 0) & (BAND - 1))
    key1 = pos(lax.broadcasted_iota(jnp.int32, (2 * BAND, BAND), 1))
    return (jnp.where(allowed, 0.0, NEG_BIG).astype(F32),
            jnp.where(key1 <= row1, 0.0, NEG_BIG).astype(F32))


def _attn_call(q3, k3, v3):
    b, s, _ = q3.shape
    pairs = GROUP_WIDTH // LANES
    in_specs = []
    args = []
    for g in range(N_GROUPS):
        for arr in (q3, k3, v3):
            in_specs.append(pl.BlockSpec((None, s, LANES),
                                         lambda bi, p, g=g: (bi, 0, g * pairs + p)))
            args.append(arr)
    piece = BAND // ATTN_CLASSES
    tables = _attn_biases(lambda p: p) + _attn_biases(
        lambda p: ATTN_CLASSES * (p & (piece - 1)) + p // piece)
    for tab in tables:
        in_specs.append(_const_spec(tab.shape))
        args.append(tab)
    return pl.pallas_call(
        _attn_kernel,
        grid=(b, pairs),
        in_specs=in_specs,
        out_specs=pl.BlockSpec((None, s, LANES), lambda bi, p: (bi, 0, p)),
        out_shape=jax.ShapeDtypeStruct((b, s, GROUP_WIDTH), F32),
        scratch_shapes=[pltpu.VMEM((s, LANES), F32)] * 6,
        compiler_params=_params(("parallel", "parallel")),
        name="l0_dilated_attention",
    )(*args)


FFN_CHUNK = 1024


def _ffn_tail(h, g_ref, w1_ref, w2_ref):
    hn = _rms(h, g_ref[...]).astype(BF16)
    d_ff = w1_ref.shape[1]
    acc = h
    for c0 in range(0, d_ff, FFN_CHUNK):
        a = jnp.dot(hn, w1_ref[:, c0:c0 + FFN_CHUNK].astype(BF16), preferred_element_type=F32)
        a = jnp.square(jnp.maximum(a, 0.0)).astype(BF16)
        acc = acc + jnp.dot(a, w2_ref[c0:c0 + FFN_CHUNK, :].astype(BF16),
                            preferred_element_type=F32)
    return acc


def _out0_kernel(x_ref, a_ref, att_ref, wout_ref, g_ref, w1_ref, w2_ref, o_ref):
    c = a_ref.shape[1]
    mix = jnp.dot(a_ref[...].astype(BF16), wout_ref[0:c, :].astype(BF16),
                  preferred_element_type=F32)
    mix = mix + jnp.dot(att_ref[...].astype(BF16), wout_ref[c:, :].astype(BF16),
                        preferred_element_type=F32)
    o_ref[...] = _ffn_tail(x_ref[...] + mix, g_ref, w1_ref, w2_ref)


def _out0_call(x2, a2, att2, w_out, g, w1, w2, layer, tm):
    t, d = x2.shape
    row = lambda i: (i, 0)
    return pl.pallas_call(
        _out0_kernel,
        grid=(t // tm,),
        in_specs=[
            pl.BlockSpec((tm, d), row),
            pl.BlockSpec((tm, a2.shape[1]), row),
            pl.BlockSpec((tm, att2.shape[1]), row),
            _layer_spec(w_out.shape, layer),
            _const_spec((1, d)),
            _layer_spec(w1.shape, 2 * layer),
            _layer_spec(w2.shape, 2 * layer),
        ],
        out_specs=pl.BlockSpec((tm, d), row),
        out_shape=jax.ShapeDtypeStruct((t, d), F32),
        compiler_params=_params(("parallel",)),
        name="l0_out_proj_ffn",
    )(x2, a2, att2, w_out, g, w1, w2)


SCONV_HALO = 8
GELU_C = math.sqrt(2.0 / math.pi)


def _gelu_tanh(x):
    return 0.5 * x * (1.0 + jnp.tanh(GELU_C * (x + 0.044715 * (x * x * x))))


def _l1_kernel(h_ref, g_ref, win_ref, ck_ref, lg_ref, lb_ref, sgw_ref, sgbt_ref, wout_ref,
               fg_ref, w1_ref, w2_ref, final_g_ref, o_ref, ybuf, *, blocks_per_seq):
    tm = h_ref.shape[0]
    i = pl.program_id(0)
    h = h_ref[...]
    hn = _rms(h, g_ref[...]).astype(BF16)

    def proj(c0, n):
        return jnp.dot(hn, win_ref[:, c0:c0 + n].astype(BF16), preferred_element_type=F32)

    gb = proj(0, SCONV_CH)
    y = proj(SCONV_CH, SCONV_CH) * proj(2 * SCONV_CH, SCONV_CH)

    @pl.when(i % blocks_per_seq == 0)
    def _():
        ybuf[0:SCONV_HALO, :] = jnp.zeros((SCONV_HALO, SCONV_CH), F32)

    ybuf[SCONV_HALO:SCONV_HALO + tm, :] = y
    conv = y * ck_ref[SCONV_WIDTH - 1:SCONV_WIDTH, :]
    for j in range(SCONV_WIDTH - 1):
        sh = SCONV_WIDTH - 1 - j
        conv = conv + ybuf[SCONV_HALO - sh:SCONV_HALO - sh + tm, :] * ck_ref[j:j + 1, :]
    ybuf[0:SCONV_HALO, :] = y[tm - SCONV_HALO:tm, :]
    c_out = (gb * conv).astype(BF16)

    u = _gelu_tanh(proj(3 * SCONV_CH, SG_CH))
    v = _gelu_tanh(proj(3 * SCONV_CH + SG_CH, SG_CH))
    v = _layer_norm(v, lg_ref[...], lb_ref[...]).astype(BF16)
    rr = lax.broadcasted_iota(jnp.int32, (CHUNK, CHUNK), 0)
    cc = lax.broadcasted_iota(jnp.int32, (CHUNK, CHUNK), 1)
    causal = rr >= cc
    d_cols = []
    for gi in range(SG_GROUPS):
        ws = jnp.where(causal, sgw_ref[gi], 0.0).astype(BF16)
        bcol = sgbt_ref[:, gi:gi + 1]
        chunks = []
        for n in range(tm // CHUNK):
            vv = v[n * CHUNK:(n + 1) * CHUNK, gi * SG_HEAD:(gi + 1) * SG_HEAD]
            chunks.append(jnp.dot(ws, vv, preferred_element_type=F32) + bcol)
        d_cols.append(jnp.concatenate(chunks, axis=0))
    d_out = (u * jnp.concatenate(d_cols, axis=1)).astype(BF16)

    mix = jnp.dot(c_out, wout_ref[0:SCONV_CH, :].astype(BF16), preferred_element_type=F32)
    mix = mix + jnp.dot(d_out, wout_ref[SCONV_CH:, :].astype(BF16),
                        preferred_element_type=F32)
    o_ref[...] = _rms(_ffn_tail(h + mix, fg_ref, w1_ref, w2_ref), final_g_ref[...])


def _l1_call(h2, g, w_in, conv_k, ln_g, ln_b, sg_w, sg_bt, w_out, layer, ffn_g, w1, w2, ffn_layer,
             final_g, seq, tm):
    t, d = h2.shape
    row = lambda i: (i, 0)
    return pl.pallas_call(
        functools.partial(_l1_kernel, blocks_per_seq=seq // tm),
        grid=(t // tm,),
        in_specs=[
            pl.BlockSpec((tm, d), row),
            _const_spec((1, d)),
            _layer_spec(w_in.shape, layer),
            _const_spec(conv_k.shape),
            _const_spec((1, SG_CH)),
            _const_spec((1, SG_CH)),
            _const_spec(sg_w.shape),
            _const_spec(sg_bt.shape),
            _layer_spec(w_out.shape, layer),
            _const_spec((1, d)),
            _layer_spec(w1.shape, ffn_layer),
            _layer_spec(w2.shape, ffn_layer),
            _const_spec((1, d)),
        ],
        out_specs=pl.BlockSpec((tm, d), row),
        out_shape=jax.ShapeDtypeStruct((t, d), F32),
        scratch_shapes=[pltpu.VMEM((SCONV_HALO + tm, SCONV_CH), F32)],
        compiler_params=_params(("arbitrary",)),
        name="l1_mixer_ffn_final",
    )(h2, g, w_in, conv_k, ln_g, ln_b, sg_w, sg_bt, w_out, ffn_g, w1, w2, final_g)


def _rope_tables(seq):
    half = HEAD_DIM // 2
    inv = ROPE_THETA ** (-jnp.arange(half, dtype=F32) / half)
    ang = jnp.arange(seq, dtype=F32)[:, None] * inv[None, :]
    cos, sin = jnp.cos(ang), jnp.sin(ang)
    reps = LANES // HEAD_DIM
    cos_t = jnp.tile(jnp.concatenate([cos, cos], axis=-1), (1, reps))
    sin_t = jnp.tile(jnp.concatenate([-sin, sin], axis=-1), (1, reps))
    return cos_t, sin_t


def kernel(x, norm_mix_g, norm_ffn_g, even_w_in, even_conv_k, even_conv_b, even_ln_g, even_ln_b,
           even_w_out, odd_w_in, odd_conv_k, odd_ln_g, odd_ln_b, odd_sg_w, odd_sg_b, odd_w_out,
           ffn_w1, ffn_w2, final_g):
    b, s, d = x.shape
    t = b * s
    x2 = x.reshape(t, d)
    cos_t, sin_t = _rope_tables(s)

    a, q, k, v = _in0_call(x2, norm_mix_g[0:1], even_w_in, 0, cos_t, sin_t,
                           jnp.broadcast_to(even_conv_k[0][:, None, :],
                                            (CONV_WIDTH, SUBLANES, CONV_CH)),
                           even_conv_b[0:1], even_ln_g[0:1], even_ln_b[0:1], s, ROWS_IN_PROJ)
    att = _attn_call(q.reshape(b, s, ATTN_WIDTH), k.reshape(b, s, ATTN_WIDTH),
                     v.reshape(b, s, ATTN_WIDTH))
    h = _out0_call(x2, a, att.reshape(t, GROUP_WIDTH), even_w_out, norm_ffn_g[0:1],
                   ffn_w1, ffn_w2, 0, ROWS_TAIL)
    h = _l1_call(h, norm_mix_g[1:2], odd_w_in, odd_conv_k[0], odd_ln_g[0:1], odd_ln_b[0:1],
                 odd_sg_w[0], odd_sg_b[0].T, odd_w_out.astype(BF16), 0, norm_ffn_g[1:2], ffn_w1, ffn_w2, 1,
                 final_g[None, :], s, ROWS_TAIL)
    return h.reshape(b, s, d)
```
